```python
import math
import jax, jax.numpy as jnp
from jax import lax
import numpy as np

D_MODEL = 1024
BATCH = 32
SEQ = 256
DEPTH = 4
DEC_BATCH = 8
DEC_SEQ = 4096
PAST_LEN = 256

GRID_W = 64
MIX_WIDTH = D_MODEL
POOL_WIDTH = MIX_WIDTH // 4
POOL_WINDOWS = (2, 4, 8, 16)
POOL_GROUPS = len(POOL_WINDOWS)
POOL_GROUP_DIM = POOL_WIDTH // POOL_GROUPS
CONV_WIDTH = MIX_WIDTH // 4
GQA_WIDTH = MIX_WIDTH // 4
GQA_HEAD_DIM = 64
GQA_HEADS = GQA_WIDTH // GQA_HEAD_DIM
GQA_KV_HEADS = 2
GQA_GROUP = GQA_HEADS // GQA_KV_HEADS
DIFF_WIDTH = MIX_WIDTH - POOL_WIDTH - CONV_WIDTH - GQA_WIDTH
DIFF_HEADS = 4
DIFF_HEAD_DIM = DIFF_WIDTH // DIFF_HEADS // 2
FFN_DIM = 2816
N_MOD = 9
ROPE_THETA = 10000.0
Q_BLOCK = 128
NORM_EPS = 1e-6

IN_SIZES = (POOL_WIDTH,
            CONV_WIDTH, CONV_WIDTH, CONV_WIDTH,
            GQA_HEADS * GQA_HEAD_DIM,
            GQA_KV_HEADS * GQA_HEAD_DIM,
            GQA_KV_HEADS * GQA_HEAD_DIM,
            DIFF_HEADS * 2 * DIFF_HEAD_DIM,
            DIFF_HEADS * 2 * DIFF_HEAD_DIM,
            DIFF_HEADS * 2 * DIFF_HEAD_DIM)
MIX_IN = int(sum(IN_SIZES))
IN_SPLITS = [int(v) for v in np.cumsum(IN_SIZES)[:-1]]

kernel_name = "hybrid_pool_conv_gqa_diffattn_dit_step"


def rmsnorm(x, g):
    xf = x.astype(jnp.float32)
    y = xf * lax.rsqrt(jnp.mean(xf * xf, axis=-1, keepdims=True) + NORM_EPS)
    return (y * g.astype(jnp.float32)).astype(x.dtype)


def axial_rope(seq, dim):
    rows = seq // GRID_W
    row = jnp.broadcast_to(jnp.arange(rows)[:, None], (rows, GRID_W)).reshape(seq).astype(jnp.float32)
    col = jnp.broadcast_to(jnp.arange(GRID_W)[None, :], (rows, GRID_W)).reshape(seq).astype(jnp.float32)
    quarter = dim // 4
    inv = 1.0 / (ROPE_THETA ** (jnp.arange(quarter, dtype=jnp.float32) / quarter))
    ar = row[:, None] * inv
    ac = col[:, None] * inv
    ang = jnp.concatenate([ar, ar, ac, ac], axis=-1)
    return jnp.cos(ang), jnp.sin(ang)


def apply_rope(x, cs):
    cos, sin = cs
    x1, x2, x3, x4 = jnp.split(x, 4, axis=-1)
    rot = jnp.concatenate([-x2, x1, -x4, x3], axis=-1)
    return (x.astype(jnp.float32) * cos + rot.astype(jnp.float32) * sin).astype(x.dtype)


def modulate(h, shift, scale):
    return h * (1.0 + scale) + shift


def swiglu(h, w_in, w_out):
    g, u = jnp.split(h @ w_in, 2, axis=-1)
    return (jax.nn.silu(g) * u) @ w_out


def pool_mix(u, w_grp, scale):
    b, s, _ = u.shape
    uf = u.astype(jnp.float32)
    cs = jnp.concatenate([jnp.zeros_like(uf[:, :1]), jnp.cumsum(uf, axis=1)], axis=1)
    t = jnp.arange(s)
    u_groups = jnp.split(uf, POOL_GROUPS, axis=-1)
    cs_groups = jnp.split(cs, POOL_GROUPS, axis=-1)
    outs = []
    for w, ug, cg in zip(POOL_WINDOWS, u_groups, cs_groups):
        lo = jnp.clip(t - w // 2, 0, s)
        hi = jnp.clip(t + w // 2, 0, s)
        total = jnp.take(cg, hi, axis=1) - jnp.take(cg, lo, axis=1)
        cnt = (hi - lo).astype(jnp.float32)[None, :, None]
        outs.append(total / cnt - ug)
    p = jnp.stack(outs, axis=2).astype(u.dtype)
    y = jnp.einsum('bsgc,gcd->bsgd', p, w_grp).reshape(b, s, POOL_WIDTH)
    return y * scale


def conv_mix(h, bg, cg, w):
    u = cg * h
    up = jnp.pad(u, ((0, 0), (1, 1), (0, 0)))
    conv = up[:, :-2] * w[0] + up[:, 1:-1] * w[1] + up[:, 2:] * w[2]
    return bg * conv


def gqa_attend(q, k, v):
    b, hk, g, s, d = q.shape
    nb = s // Q_BLOCK
    qb = q.reshape(b, hk, g, nb, Q_BLOCK, d).transpose(3, 0, 1, 2, 4, 5)
    sc_mult = d ** -0.5

    def one(qi):
        sc = jnp.einsum('bhgqd,bhkd->bhgqk', qi, k).astype(jnp.float32) * sc_mult
        p = jax.nn.softmax(sc, axis=-1).astype(v.dtype)
        return jnp.einsum('bhgqk,bhkd->bhgqd', p, v)

    o = lax.map(one, qb)
    return o.transpose(1, 2, 3, 0, 4, 5).reshape(b, hk, g, s, d)


def diff_attend(q, k, v, lam):
    b, h, _, s, d = q.shape
    nb = s // Q_BLOCK
    qb = q.reshape(b, h, 2, nb, Q_BLOCK, d).transpose(3, 0, 1, 2, 4, 5)
    sc_mult = d ** -0.5

    def one(qi):
        sc = jnp.einsum('bhcqd,bhckd->bhcqk', qi, k).astype(jnp.float32) * sc_mult
        p = jax.nn.softmax(sc, axis=-1)
        a = (p[:, :, 0] - lam * p[:, :, 1]).astype(v.dtype)
        return jnp.einsum('bhqk,bhkd->bhqd', a, v)

    o = lax.map(one, qb)
    return o.transpose(1, 2, 0, 3, 4).reshape(b, h, s, v.shape[-1])


def token_mix(h, p, lam_init, ropes, ctx):
    b, s, _ = h.shape
    proj = h @ p['w_mix_in']
    u_pool, h_conv, b_conv, c_conv, q_g, k_g, v_g, q_d, k_d, v_d = jnp.split(proj, IN_SPLITS, axis=-1)

    y_pool = pool_mix(u_pool, p['pool_w'], p['pool_scale'])
    y_conv = conv_mix(h_conv, b_conv, c_conv, p['conv_w'])

    q_g = rmsnorm(q_g.reshape(b, s, GQA_HEADS, GQA_HEAD_DIM), p['gqa_q_norm']).transpose(0, 2, 1, 3)
    k_g = rmsnorm(k_g.reshape(b, s, GQA_KV_HEADS, GQA_HEAD_DIM), p['gqa_k_norm']).transpose(0, 2, 1, 3)
    v_g = v_g.reshape(b, s, GQA_KV_HEADS, GQA_HEAD_DIM).transpose(0, 2, 1, 3)
    q_d = q_d.reshape(b, s, DIFF_HEADS, 2, DIFF_HEAD_DIM).transpose(0, 2, 3, 1, 4)
    k_d = k_d.reshape(b, s, DIFF_HEADS, 2, DIFF_HEAD_DIM).transpose(0, 2, 3, 1, 4)
    v_d = v_d.reshape(b, s, DIFF_HEADS, 2 * DIFF_HEAD_DIM).transpose(0, 2, 1, 3)

    kv_out = (k_g, v_g,
              k_d.transpose(0, 1, 3, 2, 4).reshape(b, DIFF_HEADS, s, 2 * DIFF_HEAD_DIM), v_d)

    if ctx is None:
        keys_g, vals_g, keys_d, vals_d = k_g, v_g, k_d, v_d
    else:
        rope_g, rope_d = ropes
        ck_g, cv_g, ck_d, cv_d = ctx
        n_ctx = ck_d.shape[2]
        ck_d = ck_d.reshape(b, DIFF_HEADS, n_ctx, 2, DIFF_HEAD_DIM).transpose(0, 1, 3, 2, 4)
        q_g = apply_rope(q_g, rope_g)
        q_d = apply_rope(q_d, rope_d)
        keys_g = jnp.concatenate([ck_g, apply_rope(k_g, rope_g)], axis=2)
        vals_g = jnp.concatenate([cv_g, v_g], axis=2)
        keys_d = jnp.concatenate([ck_d, apply_rope(k_d, rope_d)], axis=3)
        vals_d = jnp.concatenate([cv_d, v_d], axis=2)

    o_g = gqa_attend(q_g.reshape(b, GQA_KV_HEADS, GQA_GROUP, s, GQA_HEAD_DIM), keys_g, vals_g)
    y_gqa = o_g.reshape(b, GQA_HEADS, s, GQA_HEAD_DIM).transpose(0, 2, 1, 3).reshape(b, s, GQA_WIDTH)

    lv = p['diff_lambda'].astype(jnp.float32)
    lam = jnp.exp(jnp.sum(lv[0] * lv[1])) - jnp.exp(jnp.sum(lv[2] * lv[3])) + lam_init
    o_d = diff_attend(q_d, keys_d, vals_d, lam)
    o_d = rmsnorm(o_d, p['diff_subln']) * (1.0 - lam_init)
    y_diff = o_d.transpose(0, 2, 1, 3).reshape(b, s, DIFF_WIDTH)

    y = jnp.concatenate([y_pool, y_conv, y_gqa, y_diff], axis=-1) @ p['w_mix_out']
    return y, kv_out


def layer(x, cond, p, lam_init, ropes, ctx):
    mods = (jax.nn.silu(cond) @ p['w_ada'] + p['b_ada']).reshape(cond.shape[0], N_MOD, 1, D_MODEL)
    sh1, sc1, g1, sh2, sc2, g2, sh3, sc3, g3 = [mods[:, i] for i in range(N_MOD)]
    h = modulate(rmsnorm(x, p['norm_ffn1']), sh1, sc1)
    x = x + 0.5 * g1 * swiglu(h, p['w_ffn1_in'], p['w_ffn1_out'])
    h = modulate(rmsnorm(x, p['norm_mix']), sh2, sc2)
    y, kv = token_mix(h, p, lam_init, ropes, ctx)
    x = x + g2 * y
    h = modulate(rmsnorm(x, p['norm_ffn2']), sh3, sc3)
    x = x + 0.5 * g3 * swiglu(h, p['w_ffn2_in'], p['w_ffn2_out'])
    return x, kv


def setup_inputs(seed: int = 0) -> dict:
    key = jax.random.key(seed)
    ks = jax.random.split(key, 32)
    f32 = jnp.float32
    nrm = lambda k, shape, s: (jax.random.normal(k, shape, f32) * s)
    D, L = D_MODEL, DEPTH
    return {
        "x_prompt": nrm(ks[0], (BATCH, SEQ, D), 1.0),
        "x_sample": nrm(ks[1], (DEC_BATCH, DEC_SEQ, D), 1.0),
        "cache_gqa_k": nrm(ks[2], (DEC_BATCH, L, GQA_KV_HEADS, PAST_LEN, GQA_HEAD_DIM), 1.0),
        "cache_gqa_v": nrm(ks[3], (DEC_BATCH, L, GQA_KV_HEADS, PAST_LEN, GQA_HEAD_DIM), 1.0),
        "cache_diff_k": nrm(ks[4], (DEC_BATCH, L, DIFF_HEADS, PAST_LEN, 2 * DIFF_HEAD_DIM), 1.0),
        "cache_diff_v": nrm(ks[5], (DEC_BATCH, L, DIFF_HEADS, PAST_LEN, 2 * DIFF_HEAD_DIM), 1.0),
        "c": nrm(ks[6], (DEC_BATCH, D), 1.0),
        "c_ctx": nrm(ks[7], (D,), 1.0),
        "w_ada": nrm(ks[8], (L, D, N_MOD * D), 0.5 * D ** -0.5),
        "b_ada": nrm(ks[9], (L, N_MOD * D), 0.02),
        "norm_ffn1": 1.0 + nrm(ks[10], (L, D), 0.05),
        "norm_mix": 1.0 + nrm(ks[11], (L, D), 0.05),
        "norm_ffn2": 1.0 + nrm(ks[12], (L, D), 0.05),
        "w_ffn1_in": nrm(ks[13], (L, D, 2 * FFN_DIM), D ** -0.5),
        "w_ffn1_out": nrm(ks[14], (L, FFN_DIM, D), FFN_DIM ** -0.5),
        "w_ffn2_in": nrm(ks[15], (L, D, 2 * FFN_DIM), D ** -0.5),
        "w_ffn2_out": nrm(ks[16], (L, FFN_DIM, D), FFN_DIM ** -0.5),
        "w_mix_in": nrm(ks[17], (L, D, MIX_IN), D ** -0.5),
        "pool_w": nrm(ks[18], (L, POOL_GROUPS, POOL_GROUP_DIM, POOL_GROUP_DIM), POOL_GROUP_DIM ** -0.5),
        "pool_scale": 1.0 + nrm(ks[19], (L, POOL_WIDTH), 0.1),
        "conv_w": nrm(ks[20], (L, 3, CONV_WIDTH), 3 ** -0.5),
        "gqa_q_norm": 1.0 + nrm(ks[21], (L, GQA_HEAD_DIM), 0.05),
        "gqa_k_norm": 1.0 + nrm(ks[22], (L, GQA_HEAD_DIM), 0.05),
        "diff_lambda": nrm(ks[23], (L, 4, DIFF_HEAD_DIM), 0.1),
        "diff_subln": 1.0 + nrm(ks[24], (L, 2 * DIFF_HEAD_DIM), 0.05),
        "w_mix_out": nrm(ks[25], (L, MIX_WIDTH, D), MIX_WIDTH ** -0.5),
        "final_norm": 1.0 + nrm(ks[26], (D,), 0.05),
    }


def reference(x_prompt, x_sample, cache_gqa_k, cache_gqa_v, cache_diff_k, cache_diff_v, c, c_ctx,
              w_ada, b_ada, norm_ffn1, norm_mix, norm_ffn2, w_ffn1_in, w_ffn1_out, w_ffn2_in, w_ffn2_out,
              w_mix_in, pool_w, pool_scale, conv_w, gqa_q_norm, gqa_k_norm, diff_lambda, diff_subln,
              w_mix_out, final_norm):
    def params(l):
        return dict(w_ada=w_ada[l], b_ada=b_ada[l], norm_ffn1=norm_ffn1[l], norm_mix=norm_mix[l],
                    norm_ffn2=norm_ffn2[l], w_ffn1_in=w_ffn1_in[l], w_ffn1_out=w_ffn1_out[l],
                    w_ffn2_in=w_ffn2_in[l], w_ffn2_out=w_ffn2_out[l], w_mix_in=w_mix_in[l],
                    pool_w=pool_w[l], pool_scale=pool_scale[l], conv_w=conv_w[l],
                    gqa_q_norm=gqa_q_norm[l], gqa_k_norm=gqa_k_norm[l], diff_lambda=diff_lambda[l],
                    diff_subln=diff_subln[l], w_mix_out=w_mix_out[l])

    lam_inits = [0.8 - 0.6 * math.exp(-0.3 * l) for l in range(DEPTH)]

    h = x_prompt
    cond_ctx = c_ctx[None, :]
    ks_g, vs_g, ks_d, vs_d = [], [], [], []
    for l in range(DEPTH):
        h, kv = layer(h, cond_ctx, params(l), lam_inits[l], None, None)
        ks_g.append(kv[0]); vs_g.append(kv[1]); ks_d.append(kv[2]); vs_d.append(kv[3])
    y_prompt = rmsnorm(h, final_norm)
    new_gqa_k = jnp.stack(ks_g, axis=1)
    new_gqa_v = jnp.stack(vs_g, axis=1)
    new_diff_k = jnp.stack(ks_d, axis=1)
    new_diff_v = jnp.stack(vs_d, axis=1)

    s_lat = x_sample.shape[1]
    ropes = (axial_rope(s_lat, GQA_HEAD_DIM), axial_rope(s_lat, DIFF_HEAD_DIM))
    z = x_sample
    for l in range(DEPTH):
        ctx = (cache_gqa_k[:, l], cache_gqa_v[:, l], cache_diff_k[:, l], cache_diff_v[:, l])
        z, _ = layer(z, c, params(l), lam_inits[l], ropes, ctx)
    y_sample = rmsnorm(z, final_norm)

    return (y_prompt, y_sample, new_gqa_k, new_gqa_v, new_diff_k, new_diff_v)
```

```python
import functools
import math

import jax
import jax.numpy as jnp
from jax import lax
from jax.experimental import pallas as pl
from jax.experimental.pallas import tpu as pltpu

F32 = jnp.float32
BF16 = jnp.bfloat16

N_MOD = 9
NORM_EPS = 1e-6
ROPE_THETA = 10000.0
GRID_W = 64
POOL_WINDOWS = (2, 4, 8, 16)
POOL_WIDTH = 256
CONV_WIDTH = 256
GQA_HEADS = 4
GQA_KV_HEADS = 2
GQA_GROUP = GQA_HEADS // GQA_KV_HEADS
HEAD_W = 64
DIFF_HEADS = 4
DIFF_HEAD_DIM = 32
HALO = 8
COND_ROWS = 16

C_POOL, C_CONVH, C_CONVB, C_CONVC = 0, 256, 512, 768
C_QG, C_KG, C_VG, C_QD, C_KD, C_VD = 1024, 1280, 1408, 1536, 1792, 2048
MIX_IN = 2304

VMEM_LIMIT_BYTES = 56 * 1024 * 1024


def _cparams(sem):
    return pltpu.CompilerParams(dimension_semantics=sem, vmem_limit_bytes=VMEM_LIMIT_BYTES)


def _pick(n, cands):
    for c in cands:
        if n % c == 0:
            return c
    raise ValueError(f"no tile in {cands} divides {n}")


def _dot(a, b):
    return jnp.dot(a, b, preferred_element_type=F32)


def _dot_nt(a, b):
    return lax.dot_general(a, b, (((1,), (1,)), ((), ())), preferred_element_type=F32)


def _norm_mod(x, g, shift, scale):
    ms = jnp.mean(x * x, axis=-1, keepdims=True)
    return (x * lax.rsqrt(ms + NORM_EPS) * g) * (1.0 + scale) + shift


def _ada_kernel(c_ref, w_ref, b_ref, o_ref):
    c = c_ref[...]
    a = (c * jax.nn.sigmoid(c)).astype(BF16)
    o_ref[0] = _dot(a, w_ref[0].astype(BF16)) + b_ref[0]


def _ada(cond, w_ada, b_ada):
    L, D, N = w_ada.shape
    R = cond.shape[0]
    tn = _pick(N, (1024, 512, 256, 128))
    return pl.pallas_call(
        _ada_kernel,
        grid=(L, N // tn),
        in_specs=[pl.BlockSpec((R, D), lambda l, j: (0, 0)),
                  pl.BlockSpec((1, D, tn), lambda l, j: (l, 0, j)),
                  pl.BlockSpec((1, 1, tn), lambda l, j: (l, 0, j))],
        out_specs=pl.BlockSpec((1, R, tn), lambda l, j: (l, 0, j)),
        out_shape=jax.ShapeDtypeStruct((L, R, N), F32),
        compiler_params=_cparams(("parallel", "parallel")),
        name="ada",
    )(cond, w_ada, b_ada)


def _ffn_kernel(x_ref, mod_ref, g_ref, wg_ref, wu_ref, wo_ref, o_ref, h_ref, acc_ref, *, mod_base):
    j = pl.program_id(1)

    @pl.when(j == 0)
    def _():
        m = mod_ref[0]
        h = _norm_mod(x_ref[...], g_ref[0], m[mod_base:mod_base + 1], m[mod_base + 1:mod_base + 2])
        h_ref[...] = h.astype(BF16)
        acc_ref[...] = jnp.zeros_like(acc_ref)

    h = h_ref[...]
    g = _dot(h, wg_ref[0])
    u = _dot(h, wu_ref[0])
    a = (g * jax.nn.sigmoid(g) * u).astype(BF16)
    acc_ref[...] += _dot(a, wo_ref[0])

    @pl.when(j == pl.num_programs(1) - 1)
    def _():
        gate = mod_ref[0][mod_base + 2:mod_base + 3]
        o_ref[...] = x_ref[...] + (0.5 * gate) * acc_ref[...]


def _ffn(x, mods, norm_g, w_in, w_out, *, layer, mod_base, row0, rows_per_cond):
    T, D = x.shape
    F = w_out.shape[1]
    tm = _pick(T, (512, 256, 128))
    tf = _pick(F, (1408, 1024, 512, 256, 128))
    nf = F // tf
    mrow = lambda i, j: (layer * COND_ROWS + row0 + (i * tm) // rows_per_cond, 0, 0)
    return pl.pallas_call(
        functools.partial(_ffn_kernel, mod_base=mod_base),
        grid=(T // tm, nf),
        in_specs=[pl.BlockSpec((tm, D), lambda i, j: (i, 0)),
                  pl.BlockSpec((1, N_MOD, D), mrow),
                  pl.BlockSpec((1, 1, D), lambda i, j: (layer, 0, 0)),
                  pl.BlockSpec((1, D, tf), lambda i, j: (layer, 0, j)),
                  pl.BlockSpec((1, D, tf), lambda i, j: (layer, 0, nf + j)),
                  pl.BlockSpec((1, tf, D), lambda i, j: (layer, j, 0))],
        out_specs=pl.BlockSpec((tm, D), lambda i, j: (i, 0)),
        out_shape=jax.ShapeDtypeStruct((T, D), F32),
        scratch_shapes=[pltpu.VMEM((tm, D), BF16), pltpu.VMEM((tm, D), F32)],
        compiler_params=_cparams(("parallel", "arbitrary")),
        name="ffn",
    )(x, mods, norm_g, w_in, w_in, w_out)


def _mixin_kernel(x_ref, mod_ref, g_ref, w_ref, o_ref):
    m = mod_ref[0]
    h = _norm_mod(x_ref[...], g_ref[0], m[3:4], m[4:5])
    o_ref[...] = _dot(h.astype(BF16), w_ref[0])


def _mixin(x, mods, norm_g, w_mix_in, *, layer, row0, rows_per_cond):
    T, D = x.shape
    N = w_mix_in.shape[2]
    tm = _pick(T, (512, 256, 128))
    mrow = lambda i: (layer * COND_ROWS + row0 + (i * tm) // rows_per_cond, 0, 0)
    return pl.pallas_call(
        _mixin_kernel,
        grid=(T // tm,),
        in_specs=[pl.BlockSpec((tm, D), lambda i: (i, 0)),
                  pl.BlockSpec((1, N_MOD, D), mrow),
                  pl.BlockSpec((1, 1, D), lambda i: (layer, 0, 0)),
                  pl.BlockSpec((1, D, N), lambda i: (layer, 0, 0))],
        out_specs=pl.BlockSpec((tm, N), lambda i: (i, 0)),
        out_shape=jax.ShapeDtypeStruct((T, N), F32),
        compiler_params=_cparams(("parallel",)),
        name="mixin",
    )(x, mods, norm_g, w_mix_in)


def _rope_tm(x, cos, sin_up, sin_dn, quarter):
    outs = []
    for c0 in range(0, x.shape[1], 128):
        xb = x[:, c0:c0 + 128]
        up = pltpu.roll(xb, 128 - quarter, 1)
        dn = pltpu.roll(xb, quarter, 1)
        outs.append(xb * cos + up * sin_up + dn * sin_dn)
    return outs[0] if len(outs) == 1 else jnp.concatenate(outs, axis=1)


def _head_rms_scale(x):
    return lax.rsqrt(jnp.mean(x * x, axis=-1, keepdims=True) + NORM_EPS)


def _seqmix_kernel(*refs, ts, seq_len, rope, kv_f32):
    it = iter(refs)
    prev_ref, cur_ref, next_ref = next(it), next(it), next(it)
    pw_ref, ps_ref, cw_ref, qn_ref, kn_ref = next(it), next(it), next(it), next(it), next(it)
    if rope:
        cg_ref, sgu_ref, sgd_ref, cd_ref, sdu_ref, sdd_ref = (next(it) for _ in range(6))
    ypc_ref, qg_ref, qd_ref, kg_ref, vg_ref, kd_ref, vd_ref = (next(it) for _ in range(7))

    j = pl.program_id(1)
    nt = pl.num_programs(1)
    has_prev = (j > 0).astype(F32)
    has_next = (j < nt - 1).astype(F32)

    def ext(c0):
        return jnp.concatenate([prev_ref[:, c0:c0 + 256] * has_prev,
                                cur_ref[:, c0:c0 + 256],
                                next_ref[:, c0:c0 + 256] * has_next], axis=0)

    ue = ext(C_POOL)
    n = ts + 2 * HALO
    up = lambda x, s: pltpu.roll(x, n - s, 0)
    dn = lambda x, s: pltpu.roll(x, s, 0)
    c2 = ue + dn(ue, 1)
    c4 = up(c2, 1) + dn(c2, 1)
    c8 = up(c4, 2) + dn(c4, 2)
    c16 = up(c8, 4) + dn(c8, 4)
    lane = lax.broadcasted_iota(jnp.int32, (1, POOL_WIDTH), 1)
    mid = slice(HALO, HALO + ts)
    total = jnp.where(lane < 64, c2[mid],
                      jnp.where(lane < 128, c4[mid], jnp.where(lane < 192, c8[mid], c16[mid])))
    half = jnp.where(lane < 64, 1, jnp.where(lane < 128, 2, jnp.where(lane < 192, 4, 8)))
    pos = j * ts + lax.broadcasted_iota(jnp.int32, (ts, 1), 0)
    cnt = jnp.minimum(pos + half, seq_len) - jnp.maximum(pos - half, 0)
    u = cur_ref[:, C_POOL:C_POOL + 256]
    p = total / cnt.astype(F32) - u
    y_pool = _dot(p.astype(BF16), pw_ref[0]) * ps_ref[0]

    ce = ext(C_CONVC) * ext(C_CONVH)
    cw = cw_ref[0]
    conv = (dn(ce, 1)[mid] * cw[0:1] + ce[mid] * cw[1:2] + up(ce, 1)[mid] * cw[2:3])
    y_conv = cur_ref[:, C_CONVB:C_CONVB + 256] * conv
    ypc_ref[...] = jnp.concatenate([y_pool, y_conv], axis=1).astype(BF16)

    q = cur_ref[:, C_QG:C_QG + 256]
    k = cur_ref[:, C_KG:C_KG + 128]
    qw = q * qn_ref[0]
    kw = k * kn_ref[0]
    if rope:
        qw = _rope_tm(qw, cg_ref[...], sgu_ref[...], sgd_ref[...], HEAD_W // 4)
        kw = _rope_tm(kw, cg_ref[...], sgu_ref[...], sgd_ref[...], HEAD_W // 4)
    sm_g = HEAD_W ** -0.5
    for h in range(GQA_HEADS):
        sl = slice(h * HEAD_W, (h + 1) * HEAD_W)
        qg_ref[h] = (qw[:, sl] * (_head_rms_scale(q[:, sl]) * sm_g)).astype(BF16)
    v = cur_ref[:, C_VG:C_VG + 128]
    for h in range(GQA_KV_HEADS):
        sl = slice(h * HEAD_W, (h + 1) * HEAD_W)
        kh = kw[:, sl] * _head_rms_scale(k[:, sl])
        if kv_f32:
            kg_ref[0, h] = kh
            vg_ref[0, h] = v[:, sl]
        else:
            kg_ref[h] = kh.astype(BF16)
            vg_ref[h] = v[:, sl].astype(BF16)

    qd = cur_ref[:, C_QD:C_QD + 256]
    kd = cur_ref[:, C_KD:C_KD + 256]
    vd = cur_ref[:, C_VD:C_VD + 256]
    if rope:
        qd = _rope_tm(qd, cd_ref[...], sdu_ref[...], sdd_ref[...], DIFF_HEAD_DIM // 4)
        kdr = _rope_tm(kd, cd_ref[...], sdu_ref[...], sdd_ref[...], DIFF_HEAD_DIM // 4)
    else:
        kdr = kd
    qd = qd * (DIFF_HEAD_DIM ** -0.5)
    lane64 = lax.broadcasted_iota(jnp.int32, (1, HEAD_W), 1)
    for h in range(DIFF_HEADS):
        sl = slice(h * HEAD_W, (h + 1) * HEAD_W)
        qh = qd[:, sl]
        qd_ref[2 * h] = jnp.where(lane64 < DIFF_HEAD_DIM, qh, 0.0).astype(BF16)
        qd_ref[2 * h + 1] = jnp.where(lane64 >= DIFF_HEAD_DIM, qh, 0.0).astype(BF16)
        if kv_f32:
            kd_ref[0, h] = kdr[:, sl]
            vd_ref[0, h] = vd[:, sl]
        else:
            kd_ref[h] = kdr[:, sl].astype(BF16)
            vd_ref[h] = vd[:, sl].astype(BF16)


def _seqmix(proj, pool_w_bd, pool_scale, conv_w, qn, kn, ropes, *, layer, batch, seq_len, kv_f32):
    T = proj.shape[0]
    ts = _pick(seq_len, (512, 256, 128))
    nt = seq_len // ts
    hb = ts // HALO
    rope = ropes is not None
    tile = lambda b, j: b * nt + j
    in_specs = [
        pl.BlockSpec((HALO, 1024), lambda b, j: (jnp.maximum(tile(b, j) * hb - 1, 0), 0)),
        pl.BlockSpec((ts, MIX_IN), lambda b, j: (tile(b, j), 0)),
        pl.BlockSpec((HALO, 1024), lambda b, j: (jnp.minimum((tile(b, j) + 1) * hb, T // HALO - 1), 0)),
        pl.BlockSpec((1, 256, 256), lambda b, j: (layer, 0, 0)),
        pl.BlockSpec((1, 1, 256), lambda b, j: (layer, 0, 0)),
        pl.BlockSpec((1, 3, 256), lambda b, j: (layer, 0, 0)),
        pl.BlockSpec((1, 1, 256), lambda b, j: (layer, 0, 0)),
        pl.BlockSpec((1, 1, 128), lambda b, j: (layer, 0, 0)),
    ]
    args = [proj, proj, proj, pool_w_bd, pool_scale, conv_w, qn, kn]
    if rope:
        in_specs += [pl.BlockSpec((ts, 128), lambda b, j: (j, 0))] * 6
        args += list(ropes)
    hm = lambda h: pl.BlockSpec((h, ts, HEAD_W), lambda b, j: (0, tile(b, j), 0))
    out_specs = [pl.BlockSpec((ts, 512), lambda b, j: (tile(b, j), 0)), hm(GQA_HEADS), hm(2 * DIFF_HEADS)]
    out_shape = [jax.ShapeDtypeStruct((T, 512), BF16),
                 jax.ShapeDtypeStruct((GQA_HEADS, T, HEAD_W), BF16),
                 jax.ShapeDtypeStruct((2 * DIFF_HEADS, T, HEAD_W), BF16)]
    for h in (GQA_KV_HEADS, GQA_KV_HEADS, DIFF_HEADS, DIFF_HEADS):
        if kv_f32:
            out_specs.append(pl.BlockSpec((1, h, ts, HEAD_W), lambda b, j: (b, 0, j, 0)))
            out_shape.append(jax.ShapeDtypeStruct((batch, h, seq_len, HEAD_W), F32))
        else:
            out_specs.append(hm(h))
            out_shape.append(jax.ShapeDtypeStruct((h, T, HEAD_W), BF16))
    return pl.pallas_call(
        functools.partial(_seqmix_kernel, ts=ts, seq_len=seq_len, rope=rope, kv_f32=kv_f32),
        grid=(batch, nt),
        in_specs=in_specs,
        out_specs=out_specs,
        out_shape=out_shape,
        compiler_params=_cparams(("parallel", "parallel")),
        name="seqmix",
    )(*args)


def _attend_kernel(*refs, has_ctx, kv_f32, lam_init):
    it = iter(refs)
    qg_ref, qd_ref, kg_ref, vg_ref, kd_ref, vd_ref = (next(it) for _ in range(6))
    if has_ctx:
        ckg_ref, cvg_ref, ckd_ref, cvd_ref = (next(it) for _ in range(4))
    lam_ref, sub_ref = next(it), next(it)
    yg_ref, yd_ref = next(it), next(it)

    def kv(ref, h):
        return ref[0, h].astype(BF16) if kv_f32 else ref[h]

    def scores(q, k_ref, ck_ref, h):
        s = [_dot_nt(q, kv(k_ref, h))]
        if has_ctx:
            s.append(_dot_nt(q, ck_ref[0, 0, h].astype(BF16)))
        return s

    def softmax_parts(s):
        m = functools.reduce(jnp.maximum, [jnp.max(x, axis=-1, keepdims=True) for x in s])
        p = [jnp.exp(x - m) for x in s]
        l = functools.reduce(jnp.add, [jnp.sum(x, axis=-1, keepdims=True) for x in p])
        return p, l

    def pv(p, v_ref, cv_ref, h):
        o = _dot(p[0].astype(BF16), kv(v_ref, h))
        if has_ctx:
            o = o + _dot(p[1].astype(BF16), cv_ref[0, 0, h].astype(BF16))
        return o

    outs = []
    for h in range(GQA_HEADS):
        kh = h // GQA_GROUP
        p, l = softmax_parts(scores(qg_ref[h], kg_ref, ckg_ref if has_ctx else None, kh))
        outs.append(pv(p, vg_ref, cvg_ref if has_ctx else None, kh) / l)
    yg_ref[...] = jnp.concatenate(outs, axis=1).astype(BF16)

    lv = lam_ref[0]
    lam = (jnp.exp(jnp.sum(lv[0:1] * lv[1:2], axis=-1, keepdims=True))
           - jnp.exp(jnp.sum(lv[2:3] * lv[3:4], axis=-1, keepdims=True)) + lam_init)
    outs = []
    for h in range(DIFF_HEADS):
        ck = ckd_ref if has_ctx else None
        p1, l1 = softmax_parts(scores(qd_ref[2 * h], kd_ref, ck, h))
        p2, l2 = softmax_parts(scores(qd_ref[2 * h + 1], kd_ref, ck, h))
        r1 = 1.0 / l1
        r2 = lam / l2
        a = [x1 * r1 - x2 * r2 for x1, x2 in zip(p1, p2)]
        o = pv(a, vd_ref, cvd_ref if has_ctx else None, h)
        o = o * _head_rms_scale(o) * sub_ref[0]
        outs.append(o * (1.0 - lam_init))
    yd_ref[...] = jnp.concatenate(outs, axis=1).astype(BF16)


def _attend(qg, qd, kg, vg, kd, vd, ctx, diff_lambda, diff_subln, *, layer, batch, seq_len, lam_init):
    T = qg.shape[1]
    has_ctx = ctx is not None
    kv_f32 = not has_ctx
    tq = _pick(seq_len, (256, 128))
    nq = seq_len // tq
    qspec = lambda h: pl.BlockSpec((h, tq, HEAD_W), lambda b, i: (0, b * nq + i, 0))
    if kv_f32:
        kvspec = lambda h: pl.BlockSpec((1, h, seq_len, HEAD_W), lambda b, i: (b, 0, 0, 0))
    else:
        kvspec = lambda h: pl.BlockSpec((h, seq_len, HEAD_W), lambda b, i: (0, b, 0))
    in_specs = [qspec(GQA_HEADS), qspec(2 * DIFF_HEADS),
                kvspec(GQA_KV_HEADS), kvspec(GQA_KV_HEADS), kvspec(DIFF_HEADS), kvspec(DIFF_HEADS)]
    args = [qg, qd, kg, vg, kd, vd]
    if has_ctx:
        for c in ctx:
            _, _, h, p, w = c.shape
            in_specs.append(pl.BlockSpec((1, 1, h, p, w), lambda b, i: (b, layer, 0, 0, 0)))
            args.append(c)
    in_specs += [pl.BlockSpec((1, 4, DIFF_HEAD_DIM), lambda b, i: (layer, 0, 0)),
                 pl.BlockSpec((1, 1, HEAD_W), lambda b, i: (layer, 0, 0))]
    args += [diff_lambda, diff_subln]
    ospec = pl.BlockSpec((tq, 256), lambda b, i: (b * nq + i, 0))
    return pl.pallas_call(
        functools.partial(_attend_kernel, has_ctx=has_ctx, kv_f32=kv_f32, lam_init=lam_init),
        grid=(batch, nq),
        in_specs=in_specs,
        out_specs=[ospec, ospec],
        out_shape=[jax.ShapeDtypeStruct((T, 256), BF16)] * 2,
        compiler_params=_cparams(("parallel", "parallel")),
        name="attend",
    )(*args)


def _mixout_kernel(x_ref, mod_ref, ypc_ref, yg_ref, yd_ref, w_ref, o_ref):
    w = w_ref[0]
    y = (_dot(ypc_ref[...], w[0:512]) + _dot(yg_ref[...], w[512:768]) + _dot(yd_ref[...], w[768:1024]))
    o_ref[...] = x_ref[...] + mod_ref[0][5:6] * y


def _mixout(x, mods, ypc, yg, yd, w_mix_out, *, layer, row0, rows_per_cond):
    T, D = x.shape
    tm = _pick(T, (512, 256, 128))
    mrow = lambda i: (layer * COND_ROWS + row0 + (i * tm) // rows_per_cond, 0, 0)
    return pl.pallas_call(
        _mixout_kernel,
        grid=(T // tm,),
        in_specs=[pl.BlockSpec((tm, D), lambda i: (i, 0)),
                  pl.BlockSpec((1, N_MOD, D), mrow),
                  pl.BlockSpec((tm, 512), lambda i: (i, 0)),
                  pl.BlockSpec((tm, 256), lambda i: (i, 0)),
                  pl.BlockSpec((tm, 256), lambda i: (i, 0)),
                  pl.BlockSpec((1, D, D), lambda i: (layer, 0, 0))],
        out_specs=pl.BlockSpec((tm, D), lambda i: (i, 0)),
        out_shape=jax.ShapeDtypeStruct((T, D), F32),
        compiler_params=_cparams(("parallel",)),
        name="mixout",
    )(x, mods, ypc, yg, yd, w_mix_out)


def _final_kernel(x_ref, g_ref, o_ref):
    x = x_ref[...]
    ms = jnp.mean(x * x, axis=-1, keepdims=True)
    o_ref[...] = x * lax.rsqrt(ms + NORM_EPS) * g_ref[...]


def _final_norm(x, g):
    T, D = x.shape
    tm = _pick(T, (1024, 512, 256, 128))
    return pl.pallas_call(
        _final_kernel,
        grid=(T // tm,),
        in_specs=[pl.BlockSpec((tm, D), lambda i: (i, 0)), pl.BlockSpec((1, D), lambda i: (0, 0))],
        out_specs=pl.BlockSpec((tm, D), lambda i: (i, 0)),
        out_shape=jax.ShapeDtypeStruct((T, D), F32),
        compiler_params=_cparams(("parallel",)),
        name="final_norm",
    )(x, g)


def _rope_tables(seq, dim):
    rows = seq // GRID_W
    row = jnp.broadcast_to(jnp.arange(rows)[:, None], (rows, GRID_W)).reshape(seq).astype(F32)
    col = jnp.broadcast_to(jnp.arange(GRID_W)[None, :], (rows, GRID_W)).reshape(seq).astype(F32)
    quarter = dim // 4
    inv = 1.0 / (ROPE_THETA ** (jnp.arange(quarter, dtype=F32) / quarter))
    ar = row[:, None] * inv
    ac = col[:, None] * inv
    ang = jnp.tile(jnp.concatenate([ar, ar, ac, ac], axis=-1), (1, 128 // dim))
    chunk = (jnp.arange(128) // quarter) % 2
    cos, sin = jnp.cos(ang), jnp.sin(ang)
    return cos, jnp.where(chunk == 0, -sin, 0.0), jnp.where(chunk == 1, sin, 0.0)


def kernel(x_prompt, x_sample, cache_gqa_k, cache_gqa_v, cache_diff_k, cache_diff_v, c, c_ctx, w_ada, b_ada, norm_ffn1, norm_mix, norm_ffn2, w_ffn1_in, w_ffn1_out, w_ffn2_in, w_ffn2_out, w_mix_in, pool_w, pool_scale, conv_w, gqa_q_norm, gqa_k_norm, diff_lambda, diff_subln, w_mix_out, final_norm):
    bp, sp, D = x_prompt.shape
    bs, ss, _ = x_sample.shape
    L = w_ada.shape[0]
    assert 1 + bs <= COND_ROWS and ss % GRID_W == 0

    cond = jnp.zeros((COND_ROWS, D), F32).at[0].set(c_ctx).at[1:1 + bs].set(c)
    mods = _ada(cond, w_ada, b_ada[:, None, :]).reshape(L * COND_ROWS, N_MOD, D)

    w1i, w1o = w_ffn1_in.astype(BF16), w_ffn1_out.astype(BF16)
    w2i, w2o = w_ffn2_in.astype(BF16), w_ffn2_out.astype(BF16)
    wmi, wmo = w_mix_in.astype(BF16), w_mix_out.astype(BF16)
    g = pool_w.shape[1]
    pw_bd = jnp.einsum('lgcd,gh->lgchd', pool_w, jnp.eye(g, dtype=F32)).reshape(L, POOL_WIDTH, POOL_WIDTH).astype(BF16)
    n1, nm, n2 = norm_ffn1[:, None, :], norm_mix[:, None, :], norm_ffn2[:, None, :]
    ps = pool_scale[:, None, :]
    qn = jnp.tile(gqa_q_norm, (1, GQA_HEADS))[:, None, :]
    kn = jnp.tile(gqa_k_norm, (1, GQA_KV_HEADS))[:, None, :]
    sub = diff_subln[:, None, :]
    ropes = _rope_tables(ss, HEAD_W) + _rope_tables(ss, DIFF_HEAD_DIM)
    ctx = (cache_gqa_k, cache_gqa_v, cache_diff_k, cache_diff_v)
    lam_inits = [0.8 - 0.6 * math.exp(-0.3 * l) for l in range(L)]

    def run_stream(x, batch, seq_len, row0, rows_per_cond, is_ctx_stream):
        kvs = []
        for l in range(L):
            cond_kw = dict(layer=l, row0=row0, rows_per_cond=rows_per_cond)
            x = _ffn(x, mods, n1, w1i, w1o, mod_base=0, **cond_kw)
            proj = _mixin(x, mods, nm, wmi, **cond_kw)
            ypc, qg, qd, kg, vg, kd, vd = _seqmix(
                proj, pw_bd, ps, conv_w, qn, kn, None if is_ctx_stream else ropes,
                layer=l, batch=batch, seq_len=seq_len, kv_f32=is_ctx_stream)
            yg, yd = _attend(qg, qd, kg, vg, kd, vd, None if is_ctx_stream else ctx, diff_lambda, sub,
                             layer=l, batch=batch, seq_len=seq_len, lam_init=lam_inits[l])
            x = _mixout(x, mods, ypc, yg, yd, wmo, **cond_kw)
            x = _ffn(x, mods, n2, w2i, w2o, mod_base=6, **cond_kw)
            kvs.append((kg, vg, kd, vd))
        return _final_norm(x, final_norm[None, :]), kvs

    yp, kvs = run_stream(x_prompt.reshape(bp * sp, D), bp, sp, 0, bp * sp, True)
    ys, _ = run_stream(x_sample.reshape(bs * ss, D), bs, ss, 1, ss, False)
    new_kv = [jnp.stack([kv[i] for kv in kvs], axis=1) for i in range(4)]
    return (yp.reshape(bp, sp, D), ys.reshape(bs, ss, D), *new_kv)
```

```python
import functools
import math

import jax
import jax.numpy as jnp
from jax import lax
from jax.experimental import pallas as pl
from jax.experimental.pallas import tpu as pltpu

F32 = jnp.float32
BF16 = jnp.bfloat16

N_MOD = 9
NORM_EPS = 1e-6
ROPE_THETA = 10000.0
GRID_W = 64
POOL_WINDOWS = (2, 4, 8, 16)
POOL_WIDTH = 256
CONV_WIDTH = 256
GQA_HEADS = 4
GQA_KV_HEADS = 2
GQA_GROUP = GQA_HEADS // GQA_KV_HEADS
HEAD_W = 64
DIFF_HEADS = 4
DIFF_HEAD_DIM = 32
HALO = 8
COND_ROWS = 16
LOG2E = math.log2(math.e)
SCORE_CHUNK = 512

C_POOL, C_CONVH, C_CONVB, C_CONVC = 0, 256, 512, 768
C_QG, C_KG, C_VG, C_QD, C_KD, C_VD = 1024, 1280, 1408, 1536, 1792, 2048
MIX_IN = 2304

VMEM_LIMIT_BYTES = 56 * 1024 * 1024


def _cparams(sem):
    return pltpu.CompilerParams(dimension_semantics=sem, vmem_limit_bytes=VMEM_LIMIT_BYTES)


def _pick(n, cands):
    for c in cands:
        if n % c == 0:
            return c
    raise ValueError(f"no tile in {cands} divides {n}")


def _dot(a, b):
    return jnp.dot(a, b, preferred_element_type=F32)


def _dot_nt(a, b):
    return lax.dot_general(a, b, (((1,), (1,)), ((), ())), preferred_element_type=F32)


def _norm_mod(x, g, shift, scale):
    ms = jnp.mean(x * x, axis=-1, keepdims=True)
    return (x * lax.rsqrt(ms + NORM_EPS) * g) * (1.0 + scale) + shift


def _ada_kernel(c_ref, w_ref, b_ref, o_ref):
    c = c_ref[...]
    a = (c * jax.nn.sigmoid(c)).astype(BF16)
    o_ref[0] = _dot(a, w_ref[0].astype(BF16)) + b_ref[0]


def _ada(cond, w_ada, b_ada):
    L, D, N = w_ada.shape
    R = cond.shape[0]
    tn = _pick(N, (1024, 512, 256, 128))
    return pl.pallas_call(
        _ada_kernel,
        grid=(L, N // tn),
        in_specs=[pl.BlockSpec((R, D), lambda l, j: (0, 0)),
                  pl.BlockSpec((1, D, tn), lambda l, j: (l, 0, j)),
                  pl.BlockSpec((1, 1, tn), lambda l, j: (l, 0, j))],
        out_specs=pl.BlockSpec((1, R, tn), lambda l, j: (l, 0, j)),
        out_shape=jax.ShapeDtypeStruct((L, R, N), F32),
        compiler_params=_cparams(("parallel", "parallel")),
        name="ada",
    )(cond, w_ada, b_ada)


def _ffn_kernel(x_ref, mod_ref, g_ref, wg_ref, wu_ref, wo_ref, o_ref, h_ref, acc_ref, *, mod_base):
    j = pl.program_id(1)

    @pl.when(j == 0)
    def _():
        m = mod_ref[0]
        h = _norm_mod(x_ref[...], g_ref[0], m[mod_base:mod_base + 1], m[mod_base + 1:mod_base + 2])
        h_ref[...] = h.astype(BF16)
        acc_ref[...] = jnp.zeros_like(acc_ref)

    h = h_ref[...]
    g = _dot(h, wg_ref[0])
    u = _dot(h, wu_ref[0])
    a = (g * jax.nn.sigmoid(g) * u).astype(BF16)
    acc_ref[...] += _dot(a, wo_ref[0])

    @pl.when(j == pl.num_programs(1) - 1)
    def _():
        gate = mod_ref[0][mod_base + 2:mod_base + 3]
        o_ref[...] = x_ref[...] + (0.5 * gate) * acc_ref[...]


def _ffn(x, mods, norm_g, w_in, w_out, *, layer, mod_base, row0, rows_per_cond):
    T, D = x.shape
    F = w_out.shape[1]
    tm = _pick(T, (512, 256, 128))
    tf = _pick(F, (1408, 1024, 512, 256, 128))
    nf = F // tf
    mrow = lambda i, j: (layer * COND_ROWS + row0 + (i * tm) // rows_per_cond, 0, 0)
    return pl.pallas_call(
        functools.partial(_ffn_kernel, mod_base=mod_base),
        grid=(T // tm, nf),
        in_specs=[pl.BlockSpec((tm, D), lambda i, j: (i, 0)),
                  pl.BlockSpec((1, N_MOD, D), mrow),
                  pl.BlockSpec((1, 1, D), lambda i, j: (layer, 0, 0)),
                  pl.BlockSpec((1, D, tf), lambda i, j: (layer, 0, j)),
                  pl.BlockSpec((1, D, tf), lambda i, j: (layer, 0, nf + j)),
                  pl.BlockSpec((1, tf, D), lambda i, j: (layer, j, 0))],
        out_specs=pl.BlockSpec((tm, D), lambda i, j: (i, 0)),
        out_shape=jax.ShapeDtypeStruct((T, D), F32),
        scratch_shapes=[pltpu.VMEM((tm, D), BF16), pltpu.VMEM((tm, D), F32)],
        compiler_params=_cparams(("parallel", "arbitrary")),
        name="ffn",
    )(x, mods, norm_g, w_in, w_in, w_out)


def _mixin_kernel(x_ref, mod_ref, g_ref, w_ref, o_ref):
    m = mod_ref[0]
    h = _norm_mod(x_ref[...], g_ref[0], m[3:4], m[4:5])
    o_ref[...] = _dot(h.astype(BF16), w_ref[0])


def _mixin(x, mods, norm_g, w_mix_in, *, layer, row0, rows_per_cond):
    T, D = x.shape
    N = w_mix_in.shape[2]
    tm = _pick(T, (512, 256, 128))
    mrow = lambda i: (layer * COND_ROWS + row0 + (i * tm) // rows_per_cond, 0, 0)
    return pl.pallas_call(
        _mixin_kernel,
        grid=(T // tm,),
        in_specs=[pl.BlockSpec((tm, D), lambda i: (i, 0)),
                  pl.BlockSpec((1, N_MOD, D), mrow),
                  pl.BlockSpec((1, 1, D), lambda i: (layer, 0, 0)),
                  pl.BlockSpec((1, D, N), lambda i: (layer, 0, 0))],
        out_specs=pl.BlockSpec((tm, N), lambda i: (i, 0)),
        out_shape=jax.ShapeDtypeStruct((T, N), F32),
        compiler_params=_cparams(("parallel",)),
        name="mixin",
    )(x, mods, norm_g, w_mix_in)


def _rope_tm(x, cos, sin_up, sin_dn, quarter):
    outs = []
    for c0 in range(0, x.shape[1], 128):
        xb = x[:, c0:c0 + 128]
        up = pltpu.roll(xb, 128 - quarter, 1)
        dn = pltpu.roll(xb, quarter, 1)
        outs.append(xb * cos + up * sin_up + dn * sin_dn)
    return outs[0] if len(outs) == 1 else jnp.concatenate(outs, axis=1)


def _head_rms_scale(x):
    return lax.rsqrt(jnp.mean(x * x, axis=-1, keepdims=True) + NORM_EPS)


def _seqmix_kernel(*refs, ts, seq_len, rope, kv_f32):
    it = iter(refs)
    prev_ref, cur_ref, next_ref = next(it), next(it), next(it)
    pw_ref, ps_ref, cw_ref, qn_ref, kn_ref = next(it), next(it), next(it), next(it), next(it)
    if rope:
        cg_ref, sgu_ref, sgd_ref, cd_ref, sdu_ref, sdd_ref = (next(it) for _ in range(6))
    ypc_ref, qg_ref, qd_ref, kg_ref, vg_ref, kd_ref, vd_ref = (next(it) for _ in range(7))

    j = pl.program_id(1)
    nt = pl.num_programs(1)
    has_prev = (j > 0).astype(F32)
    has_next = (j < nt - 1).astype(F32)

    def ext(c0):
        return jnp.concatenate([prev_ref[:, c0:c0 + 256] * has_prev,
                                cur_ref[:, c0:c0 + 256],
                                next_ref[:, c0:c0 + 256] * has_next], axis=0)

    ue = ext(C_POOL)
    n = ts + 2 * HALO
    up = lambda x, s: pltpu.roll(x, n - s, 0)
    dn = lambda x, s: pltpu.roll(x, s, 0)
    c2 = ue + dn(ue, 1)
    c4 = up(c2, 1) + dn(c2, 1)
    c8 = up(c4, 2) + dn(c4, 2)
    c16 = up(c8, 4) + dn(c8, 4)
    lane = lax.broadcasted_iota(jnp.int32, (1, POOL_WIDTH), 1)
    mid = slice(HALO, HALO + ts)
    total = jnp.where(lane < 64, c2[mid],
                      jnp.where(lane < 128, c4[mid], jnp.where(lane < 192, c8[mid], c16[mid])))
    half = jnp.where(lane < 64, 1, jnp.where(lane < 128, 2, jnp.where(lane < 192, 4, 8)))
    pos = j * ts + lax.broadcasted_iota(jnp.int32, (ts, 1), 0)
    cnt = jnp.minimum(pos + half, seq_len) - jnp.maximum(pos - half, 0)
    u = cur_ref[:, C_POOL:C_POOL + 256]
    p = total / cnt.astype(F32) - u
    y_pool = _dot(p.astype(BF16), pw_ref[0]) * ps_ref[0]

    ce = ext(C_CONVC) * ext(C_CONVH)
    cw = cw_ref[0]
    conv = (dn(ce, 1)[mid] * cw[0:1] + ce[mid] * cw[1:2] + up(ce, 1)[mid] * cw[2:3])
    y_conv = cur_ref[:, C_CONVB:C_CONVB + 256] * conv
    ypc_ref[...] = jnp.concatenate([y_pool, y_conv], axis=1).astype(BF16)

    q = cur_ref[:, C_QG:C_QG + 256]
    k = cur_ref[:, C_KG:C_KG + 128]
    qw = q * qn_ref[0]
    kw = k * kn_ref[0]
    if rope:
        qw = _rope_tm(qw, cg_ref[...], sgu_ref[...], sgd_ref[...], HEAD_W // 4)
        kw = _rope_tm(kw, cg_ref[...], sgu_ref[...], sgd_ref[...], HEAD_W // 4)
    sm_g = HEAD_W ** -0.5 * LOG2E
    for h in range(GQA_HEADS):
        sl = slice(h * HEAD_W, (h + 1) * HEAD_W)
        qg_ref[h] = (qw[:, sl] * (_head_rms_scale(q[:, sl]) * sm_g)).astype(BF16)
    v = cur_ref[:, C_VG:C_VG + 128]
    ones = jnp.ones((ts, HEAD_W), F32)
    k_scales = [_head_rms_scale(k[:, h * HEAD_W:(h + 1) * HEAD_W]) for h in range(GQA_KV_HEADS)]
    if kv_f32:
        for h in range(GQA_KV_HEADS):
            sl = slice(h * HEAD_W, (h + 1) * HEAD_W)
            kg_ref[0, h] = kw[:, sl] * k_scales[h]
            vg_ref[0, h] = v[:, sl]
    else:
        lane128 = lax.broadcasted_iota(jnp.int32, (1, 128), 1)
        kg_ref[...] = (kw * jnp.where(lane128 < HEAD_W, k_scales[0], k_scales[1])).T.astype(BF16)
        for h in range(GQA_KV_HEADS):
            vg_ref[h] = jnp.concatenate([v[:, h * HEAD_W:(h + 1) * HEAD_W], ones], axis=1).astype(BF16)

    qd = cur_ref[:, C_QD:C_QD + 256]
    kd = cur_ref[:, C_KD:C_KD + 256]
    vd = cur_ref[:, C_VD:C_VD + 256]
    if rope:
        qd = _rope_tm(qd, cd_ref[...], sdu_ref[...], sdd_ref[...], DIFF_HEAD_DIM // 4)
        kdr = _rope_tm(kd, cd_ref[...], sdu_ref[...], sdd_ref[...], DIFF_HEAD_DIM // 4)
    else:
        kdr = kd
    qd = qd * (DIFF_HEAD_DIM ** -0.5 * LOG2E)
    lane64 = lax.broadcasted_iota(jnp.int32, (1, HEAD_W), 1)
    if not kv_f32:
        kd_ref[...] = kdr.T.astype(BF16)
    for h in range(DIFF_HEADS):
        sl = slice(h * HEAD_W, (h + 1) * HEAD_W)
        qh = qd[:, sl]
        qd_ref[2 * h] = jnp.where(lane64 < DIFF_HEAD_DIM, qh, 0.0).astype(BF16)
        qd_ref[2 * h + 1] = jnp.where(lane64 >= DIFF_HEAD_DIM, qh, 0.0).astype(BF16)
        if kv_f32:
            kd_ref[0, h] = kdr[:, sl]
            vd_ref[0, h] = vd[:, sl]
        else:
            vd_ref[h] = jnp.concatenate([vd[:, sl], ones], axis=1).astype(BF16)


def _seqmix(proj, pool_w_bd, pool_scale, conv_w, qn, kn, ropes, *, layer, batch, seq_len, kv_f32):
    T = proj.shape[0]
    ts = _pick(seq_len, (512, 256, 128))
    nt = seq_len // ts
    hb = ts // HALO
    rope = ropes is not None
    tile = lambda b, j: b * nt + j
    in_specs = [
        pl.BlockSpec((HALO, 1024), lambda b, j: (jnp.maximum(tile(b, j) * hb - 1, 0), 0)),
        pl.BlockSpec((ts, MIX_IN), lambda b, j: (tile(b, j), 0)),
        pl.BlockSpec((HALO, 1024), lambda b, j: (jnp.minimum((tile(b, j) + 1) * hb, T // HALO - 1), 0)),
        pl.BlockSpec((1, 256, 256), lambda b, j: (layer, 0, 0)),
        pl.BlockSpec((1, 1, 256), lambda b, j: (layer, 0, 0)),
        pl.BlockSpec((1, 3, 256), lambda b, j: (layer, 0, 0)),
        pl.BlockSpec((1, 1, 256), lambda b, j: (layer, 0, 0)),
        pl.BlockSpec((1, 1, 128), lambda b, j: (layer, 0, 0)),
    ]
    args = [proj, proj, proj, pool_w_bd, pool_scale, conv_w, qn, kn]
    if rope:
        in_specs += [pl.BlockSpec((ts, 128), lambda b, j: (j, 0))] * 6
        args += list(ropes)
    hm = lambda h: pl.BlockSpec((h, ts, HEAD_W), lambda b, j: (0, tile(b, j), 0))
    out_specs = [pl.BlockSpec((ts, 512), lambda b, j: (tile(b, j), 0)), hm(GQA_HEADS), hm(2 * DIFF_HEADS)]
    out_shape = [jax.ShapeDtypeStruct((T, 512), BF16),
                 jax.ShapeDtypeStruct((GQA_HEADS, T, HEAD_W), BF16),
                 jax.ShapeDtypeStruct((2 * DIFF_HEADS, T, HEAD_W), BF16)]
    for h, is_key in ((GQA_KV_HEADS, True), (GQA_KV_HEADS, False), (DIFF_HEADS, True), (DIFF_HEADS, False)):
        if kv_f32:
            out_specs.append(pl.BlockSpec((1, h, ts, HEAD_W), lambda b, j: (b, 0, j, 0)))
            out_shape.append(jax.ShapeDtypeStruct((batch, h, seq_len, HEAD_W), F32))
        elif is_key:
            out_specs.append(pl.BlockSpec((h * HEAD_W, ts), lambda b, j: (0, tile(b, j))))
            out_shape.append(jax.ShapeDtypeStruct((h * HEAD_W, T), BF16))
        else:
            out_specs.append(pl.BlockSpec((h, ts, 2 * HEAD_W), lambda b, j: (0, tile(b, j), 0)))
            out_shape.append(jax.ShapeDtypeStruct((h, T, 2 * HEAD_W), BF16))
    return pl.pallas_call(
        functools.partial(_seqmix_kernel, ts=ts, seq_len=seq_len, rope=rope, kv_f32=kv_f32),
        grid=(batch, nt),
        in_specs=in_specs,
        out_specs=out_specs,
        out_shape=out_shape,
        compiler_params=_cparams(("parallel", "parallel")),
        name="seqmix",
    )(*args)


def _attend_kernel(*refs, has_ctx, lam_init, tq, n_keys):
    it = iter(refs)
    qg_ref, qd_ref, kg_ref, vg_ref, kd_ref, vd_ref = (next(it) for _ in range(6))
    if has_ctx:
        ckg_ref, cvg_ref, ckd_ref, cvd_ref = (next(it) for _ in range(4))
    lam_ref, sub_ref = next(it), next(it)
    yg_ref, yd_ref = next(it), next(it)
    s_ref, p_ref = next(it), next(it)

    def unit(q, k_ref, v_ref, ck_ref, cv_ref, h):
        m_acc = [None]

        def put_scores(c0, s):
            s_ref[:, c0:c0 + s.shape[1]] = s
            for b0 in range(0, s.shape[1], 128):
                blk = s[:, b0:b0 + 128]
                m_acc[0] = blk if m_acc[0] is None else jnp.maximum(m_acc[0], blk)

        if has_ctx:
            for c0 in range(0, n_keys, SCORE_CHUNK):
                put_scores(c0, _dot(q, k_ref[h * HEAD_W:(h + 1) * HEAD_W, c0:c0 + SCORE_CHUNK]))
            put_scores(n_keys, _dot_nt(q, ck_ref[0, 0, h].astype(BF16)))
            n_tot = n_keys + ck_ref.shape[3]
        else:
            put_scores(0, _dot_nt(q, k_ref[0, h].astype(BF16)))
            n_tot = n_keys
        m = jnp.max(m_acc[0], axis=-1, keepdims=True)
        for c0 in range(0, n_tot, 128):
            p_ref[:, c0:c0 + 128] = jnp.exp2(s_ref[:, c0:c0 + 128] - m).astype(BF16)
        if has_ctx:
            cv = cv_ref[0, 0, h].astype(BF16)
            o = _dot(p_ref[:, 0:n_keys], v_ref[h])
            return o + _dot(p_ref[:, n_keys:n_tot], jnp.concatenate([cv, jnp.ones_like(cv)], axis=1))
        v = v_ref[0, h].astype(BF16)
        return _dot(p_ref[...], jnp.concatenate([v, jnp.ones_like(v)], axis=1))

    def normalised(o):
        return o[:, 0:HEAD_W] / o[:, HEAD_W:HEAD_W + 1]

    outs = []
    for kh in range(GQA_KV_HEADS):
        q = qg_ref[GQA_GROUP * kh:GQA_GROUP * (kh + 1)].reshape(GQA_GROUP * tq, HEAD_W)
        o = normalised(unit(q, kg_ref, vg_ref, ckg_ref if has_ctx else None, cvg_ref if has_ctx else None, kh))
        outs += [o[g * tq:(g + 1) * tq] for g in range(GQA_GROUP)]
    yg_ref[...] = jnp.concatenate(outs, axis=1).astype(BF16)

    lv = lam_ref[0]
    lam = (jnp.exp(jnp.sum(lv[0:1] * lv[1:2], axis=-1, keepdims=True))
           - jnp.exp(jnp.sum(lv[2:3] * lv[3:4], axis=-1, keepdims=True)) + lam_init)
    outs = []
    for h in range(DIFF_HEADS):
        q = qd_ref[2 * h:2 * h + 2].reshape(2 * tq, HEAD_W)
        o = normalised(unit(q, kd_ref, vd_ref, ckd_ref if has_ctx else None, cvd_ref if has_ctx else None, h))
        o = o[0:tq] - lam * o[tq:2 * tq]
        o = o * _head_rms_scale(o) * sub_ref[0]
        outs.append(o * (1.0 - lam_init))
    yd_ref[...] = jnp.concatenate(outs, axis=1).astype(BF16)


def _attend(qg, qd, kg, vg, kd, vd, ctx, diff_lambda, diff_subln, *, layer, batch, seq_len, lam_init):
    T = qg.shape[1]
    has_ctx = ctx is not None
    tq = _pick(seq_len, (256, 128))
    nq = seq_len // tq
    qspec = lambda h: pl.BlockSpec((h, tq, HEAD_W), lambda b, i: (0, b * nq + i, 0))
    if has_ctx:
        kspec = lambda h: pl.BlockSpec((h * HEAD_W, seq_len), lambda b, i: (0, b))
        vspec = lambda h: pl.BlockSpec((h, seq_len, 2 * HEAD_W), lambda b, i: (0, b, 0))
    else:
        kspec = vspec = lambda h: pl.BlockSpec((1, h, seq_len, HEAD_W), lambda b, i: (b, 0, 0, 0))
    in_specs = [qspec(GQA_HEADS), qspec(2 * DIFF_HEADS),
                kspec(GQA_KV_HEADS), vspec(GQA_KV_HEADS), kspec(DIFF_HEADS), vspec(DIFF_HEADS)]
    args = [qg, qd, kg, vg, kd, vd]
    n_tot = seq_len
    if has_ctx:
        n_tot += ctx[0].shape[3]
        for c in ctx:
            _, _, h, p, w = c.shape
            in_specs.append(pl.BlockSpec((1, 1, h, p, w), lambda b, i: (b, layer, 0, 0, 0)))
            args.append(c)
    in_specs += [pl.BlockSpec((1, 4, DIFF_HEAD_DIM), lambda b, i: (layer, 0, 0)),
                 pl.BlockSpec((1, 1, HEAD_W), lambda b, i: (layer, 0, 0))]
    args += [diff_lambda, diff_subln]
    ospec = pl.BlockSpec((tq, 256), lambda b, i: (b * nq + i, 0))
    return pl.pallas_call(
        functools.partial(_attend_kernel, has_ctx=has_ctx, lam_init=lam_init, tq=tq, n_keys=seq_len),
        grid=(batch, nq),
        in_specs=in_specs,
        out_specs=[ospec, ospec],
        out_shape=[jax.ShapeDtypeStruct((T, 256), BF16)] * 2,
        scratch_shapes=[pltpu.VMEM((2 * tq, n_tot), F32), pltpu.VMEM((2 * tq, n_tot), BF16)],
        compiler_params=_cparams(("parallel", "parallel")),
        name="attend",
    )(*args)


def _mixout_kernel(x_ref, mod_ref, ypc_ref, yg_ref, yd_ref, w_ref, o_ref):
    w = w_ref[0]
    y = (_dot(ypc_ref[...], w[0:512]) + _dot(yg_ref[...], w[512:768]) + _dot(yd_ref[...], w[768:1024]))
    o_ref[...] = x_ref[...] + mod_ref[0][5:6] * y


def _mixout(x, mods, ypc, yg, yd, w_mix_out, *, layer, row0, rows_per_cond):
    T, D = x.shape
    tm = _pick(T, (512, 256, 128))
    mrow = lambda i: (layer * COND_ROWS + row0 + (i * tm) // rows_per_cond, 0, 0)
    return pl.pallas_call(
        _mixout_kernel,
        grid=(T // tm,),
        in_specs=[pl.BlockSpec((tm, D), lambda i: (i, 0)),
                  pl.BlockSpec((1, N_MOD, D), mrow),
                  pl.BlockSpec((tm, 512), lambda i: (i, 0)),
                  pl.BlockSpec((tm, 256), lambda i: (i, 0)),
                  pl.BlockSpec((tm, 256), lambda i: (i, 0)),
                  pl.BlockSpec((1, D, D), lambda i: (layer, 0, 0))],
        out_specs=pl.BlockSpec((tm, D), lambda i: (i, 0)),
        out_shape=jax.ShapeDtypeStruct((T, D), F32),
        compiler_params=_cparams(("parallel",)),
        name="mixout",
    )(x, mods, ypc, yg, yd, w_mix_out)


def _final_kernel(x_ref, g_ref, o_ref):
    x = x_ref[...]
    ms = jnp.mean(x * x, axis=-1, keepdims=True)
    o_ref[...] = x * lax.rsqrt(ms + NORM_EPS) * g_ref[...]


def _final_norm(x, g):
    T, D = x.shape
    tm = _pick(T, (1024, 512, 256, 128))
    return pl.pallas_call(
        _final_kernel,
        grid=(T // tm,),
        in_specs=[pl.BlockSpec((tm, D), lambda i: (i, 0)), pl.BlockSpec((1, D), lambda i: (0, 0))],
        out_specs=pl.BlockSpec((tm, D), lambda i: (i, 0)),
        out_shape=jax.ShapeDtypeStruct((T, D), F32),
        compiler_params=_cparams(("parallel",)),
        name="final_norm",
    )(x, g)


def _rope_tables(seq, dim):
    rows = seq // GRID_W
    row = jnp.broadcast_to(jnp.arange(rows)[:, None], (rows, GRID_W)).reshape(seq).astype(F32)
    col = jnp.broadcast_to(jnp.arange(GRID_W)[None, :], (rows, GRID_W)).reshape(seq).astype(F32)
    quarter = dim // 4
    inv = 1.0 / (ROPE_THETA ** (jnp.arange(quarter, dtype=F32) / quarter))
    ar = row[:, None] * inv
    ac = col[:, None] * inv
    ang = jnp.tile(jnp.concatenate([ar, ar, ac, ac], axis=-1), (1, 128 // dim))
    chunk = (jnp.arange(128) // quarter) % 2
    cos, sin = jnp.cos(ang), jnp.sin(ang)
    return cos, jnp.where(chunk == 0, -sin, 0.0), jnp.where(chunk == 1, sin, 0.0)


def kernel(x_prompt, x_sample, cache_gqa_k, cache_gqa_v, cache_diff_k, cache_diff_v, c, c_ctx, w_ada, b_ada, norm_ffn1, norm_mix, norm_ffn2, w_ffn1_in, w_ffn1_out, w_ffn2_in, w_ffn2_out, w_mix_in, pool_w, pool_scale, conv_w, gqa_q_norm, gqa_k_norm, diff_lambda, diff_subln, w_mix_out, final_norm):
    bp, sp, D = x_prompt.shape
    bs, ss, _ = x_sample.shape
    L = w_ada.shape[0]
    assert 1 + bs <= COND_ROWS and ss % GRID_W == 0

    cond = jnp.zeros((COND_ROWS, D), F32).at[0].set(c_ctx).at[1:1 + bs].set(c)
    mods = _ada(cond, w_ada, b_ada[:, None, :]).reshape(L * COND_ROWS, N_MOD, D)

    w1i, w1o = w_ffn1_in.astype(BF16), w_ffn1_out.astype(BF16)
    w2i, w2o = w_ffn2_in.astype(BF16), w_ffn2_out.astype(BF16)
    wmi, wmo = w_mix_in.astype(BF16), w_mix_out.astype(BF16)
    g = pool_w.shape[1]
    pw_bd = jnp.einsum('lgcd,gh->lgchd', pool_w, jnp.eye(g, dtype=F32)).reshape(L, POOL_WIDTH, POOL_WIDTH).astype(BF16)
    n1, nm, n2 = norm_ffn1[:, None, :], norm_mix[:, None, :], norm_ffn2[:, None, :]
    ps = pool_scale[:, None, :]
    qn = jnp.tile(gqa_q_norm, (1, GQA_HEADS))[:, None, :]
    kn = jnp.tile(gqa_k_norm, (1, GQA_KV_HEADS))[:, None, :]
    sub = diff_subln[:, None, :]
    ropes = _rope_tables(ss, HEAD_W) + _rope_tables(ss, DIFF_HEAD_DIM)
    ctx = (cache_gqa_k, cache_gqa_v, cache_diff_k, cache_diff_v)
    lam_inits = [0.8 - 0.6 * math.exp(-0.3 * l) for l in range(L)]

    def run_stream(x, batch, seq_len, row0, rows_per_cond, is_ctx_stream):
        kvs = []
        for l in range(L):
            cond_kw = dict(layer=l, row0=row0, rows_per_cond=rows_per_cond)
            x = _ffn(x, mods, n1, w1i, w1o, mod_base=0, **cond_kw)
            proj = _mixin(x, mods, nm, wmi, **cond_kw)
            ypc, qg, qd, kg, vg, kd, vd = _seqmix(
                proj, pw_bd, ps, conv_w, qn, kn, None if is_ctx_stream else ropes,
                layer=l, batch=batch, seq_len=seq_len, kv_f32=is_ctx_stream)
            yg, yd = _attend(qg, qd, kg, vg, kd, vd, None if is_ctx_stream else ctx, diff_lambda, sub,
                             layer=l, batch=batch, seq_len=seq_len, lam_init=lam_inits[l])
            x = _mixout(x, mods, ypc, yg, yd, wmo, **cond_kw)
            x = _ffn(x, mods, n2, w2i, w2o, mod_base=6, **cond_kw)
            kvs.append((kg, vg, kd, vd))
        return _final_norm(x, final_norm[None, :]), kvs

    yp, kvs = run_stream(x_prompt.reshape(bp * sp, D), bp, sp, 0, bp * sp, True)
    ys, _ = run_stream(x_sample.reshape(bs * ss, D), bs, ss, 1, ss, False)
    new_kv = [jnp.stack([kv[i] for kv in kvs], axis=1) for i in range(4)]
    return (yp.reshape(bp, sp, D), ys.reshape(bs, ss, D), *new_kv)
```

```python
import functools
import math

import jax
import jax.numpy as jnp
from jax import lax
from jax.experimental import pallas as pl
from jax.experimental.pallas import tpu as pltpu

F32 = jnp.float32
BF16 = jnp.bfloat16

N_MOD = 9
NORM_EPS = 1e-6
ROPE_THETA = 10000.0
GRID_W = 64
POOL_WINDOWS = (2, 4, 8, 16)
POOL_WIDTH = 256
CONV_WIDTH = 256
GQA_HEADS = 4
GQA_KV_HEADS = 2
GQA_GROUP = GQA_HEADS // GQA_KV_HEADS
HEAD_W = 64
DIFF_HEADS = 4
DIFF_HEAD_DIM = 32
HALO = 8
COND_ROWS = 16
LOG2E = math.log2(math.e)
SCORE_CHUNK = 512
EXP_CHUNK = 128
VT_ROWS = 80
FFN_TM = 512
FFN_CHUNK = 256

C_POOL, C_CONVH, C_CONVB, C_CONVC = 0, 256, 512, 768
C_QG, C_KG, C_VG, C_QD, C_KD, C_VD = 1024, 1280, 1408, 1536, 1792, 2048
MIX_IN = 2304

VMEM_LIMIT_BYTES = 56 * 1024 * 1024


def _cparams(sem):
    return pltpu.CompilerParams(dimension_semantics=sem, vmem_limit_bytes=VMEM_LIMIT_BYTES)


def _pick(n, cands):
    for c in cands:
        if n % c == 0:
            return c
    raise ValueError(f"no tile in {cands} divides {n}")


def _dot(a, b):
    return jnp.dot(a, b, preferred_element_type=F32)


def _norm_mod(x, g, shift, scale):
    ms = jnp.mean(x * x, axis=-1, keepdims=True)
    return (x * lax.rsqrt(ms + NORM_EPS) * g) * (1.0 + scale) + shift


def _ada_kernel(c_ref, w_ref, b_ref, o_ref):
    c = c_ref[...]
    a = (c * jax.nn.sigmoid(c)).astype(BF16)
    o_ref[0] = _dot(a, w_ref[0].astype(BF16)) + b_ref[0]


def _ada(cond, w_ada, b_ada):
    L, D, N = w_ada.shape
    R = cond.shape[0]
    tn = _pick(N, (1024, 512, 256, 128))
    return pl.pallas_call(
        _ada_kernel,
        grid=(L, N // tn),
        in_specs=[pl.BlockSpec((R, D), lambda l, j: (0, 0)),
                  pl.BlockSpec((1, D, tn), lambda l, j: (l, 0, j)),
                  pl.BlockSpec((1, 1, tn), lambda l, j: (l, 0, j))],
        out_specs=pl.BlockSpec((1, R, tn), lambda l, j: (l, 0, j)),
        out_shape=jax.ShapeDtypeStruct((L, R, N), F32),
        compiler_params=_cparams(("parallel", "parallel")),
        name="ada",
    )(cond, w_ada, b_ada)


def _ffn_kernel(x_ref, mod_ref, g_ref, wi_ref, wo_ref, o_ref, a_ref, *, mod_base, chunk):
    m = mod_ref[0]
    x = x_ref[...]
    h = _norm_mod(x, g_ref[0], m[mod_base:mod_base + 1], m[mod_base + 1:mod_base + 2]).astype(BF16)
    ffn_dim = wo_ref.shape[1]
    for c0 in range(0, ffn_dim, chunk):
        g = _dot(h, wi_ref[0, :, c0:c0 + chunk])
        u = _dot(h, wi_ref[0, :, ffn_dim + c0:ffn_dim + c0 + chunk])
        a_ref[:, c0:c0 + chunk] = (g * jax.nn.sigmoid(g) * u).astype(BF16)
    y = _dot(a_ref[...], wo_ref[0])
    o_ref[...] = x + (0.5 * m[mod_base + 2:mod_base + 3]) * y


def _ffn(x, mods, norm_g, w_in, w_out, *, layer, mod_base, row0, rows_per_cond):
    T, D = x.shape
    F = w_out.shape[1]
    tm = _pick(T, (FFN_TM, 256, 128))
    chunk = _pick(F, (FFN_CHUNK, 128))
    mrow = lambda i: (layer * COND_ROWS + row0 + (i * tm) // rows_per_cond, 0, 0)
    resident = dict(pipeline_mode=pl.Buffered(1))
    return pl.pallas_call(
        functools.partial(_ffn_kernel, mod_base=mod_base, chunk=chunk),
        grid=(T // tm,),
        in_specs=[pl.BlockSpec((tm, D), lambda i: (i, 0)),
                  pl.BlockSpec((1, N_MOD, D), mrow),
                  pl.BlockSpec((1, 1, D), lambda i: (layer, 0, 0)),
                  pl.BlockSpec((1, D, 2 * F), lambda i: (layer, 0, 0), **resident),
                  pl.BlockSpec((1, F, D), lambda i: (layer, 0, 0), **resident)],
        out_specs=pl.BlockSpec((tm, D), lambda i: (i, 0)),
        out_shape=jax.ShapeDtypeStruct((T, D), F32),
        scratch_shapes=[pltpu.VMEM((tm, F), BF16)],
        compiler_params=_cparams(("parallel",)),
        name="ffn",
    )(x, mods, norm_g, w_in, w_out)


def _mixin_kernel(x_ref, mod_ref, g_ref, w_ref, o_ref):
    m = mod_ref[0]
    h = _norm_mod(x_ref[...], g_ref[0], m[3:4], m[4:5])
    o_ref[...] = _dot(h.astype(BF16), w_ref[0])


def _mixin(x, mods, norm_g, w_mix_in, *, layer, row0, rows_per_cond):
    T, D = x.shape
    N = w_mix_in.shape[2]
    tm = _pick(T, (512, 256, 128))
    mrow = lambda i: (layer * COND_ROWS + row0 + (i * tm) // rows_per_cond, 0, 0)
    return pl.pallas_call(
        _mixin_kernel,
        grid=(T // tm,),
        in_specs=[pl.BlockSpec((tm, D), lambda i: (i, 0)),
                  pl.BlockSpec((1, N_MOD, D), mrow),
                  pl.BlockSpec((1, 1, D), lambda i: (layer, 0, 0)),
                  pl.BlockSpec((1, D, N), lambda i: (layer, 0, 0))],
        out_specs=pl.BlockSpec((tm, N), lambda i: (i, 0)),
        out_shape=jax.ShapeDtypeStruct((T, N), F32),
        compiler_params=_cparams(("parallel",)),
        name="mixin",
    )(x, mods, norm_g, w_mix_in)


def _rope_tm(x, cos, sin_up, sin_dn, quarter):
    outs = []
    for c0 in range(0, x.shape[1], 128):
        xb = x[:, c0:c0 + 128]
        up = pltpu.roll(xb, 128 - quarter, 1)
        dn = pltpu.roll(xb, quarter, 1)
        outs.append(xb * cos + up * sin_up + dn * sin_dn)
    return outs[0] if len(outs) == 1 else jnp.concatenate(outs, axis=1)


def _head_rms_scale(x):
    return lax.rsqrt(jnp.mean(x * x, axis=-1, keepdims=True) + NORM_EPS)


def _per_head(scales, width):
    lane = lax.broadcasted_iota(jnp.int32, (1, width), 1)
    out = scales[-1]
    for h in range(len(scales) - 2, -1, -1):
        out = jnp.where(lane < (h + 1) * HEAD_W, scales[h], out)
    return out


def _seqmix_kernel(*refs, ts, seq_len, rope, kv_f32):
    it = iter(refs)
    prev_ref, cur_ref, next_ref = next(it), next(it), next(it)
    pw_ref, ps_ref, cw_ref, qn_ref, kn_ref = next(it), next(it), next(it), next(it), next(it)
    if rope:
        cg_ref, sgu_ref, sgd_ref, cd_ref, sdu_ref, sdd_ref = (next(it) for _ in range(6))
    ypc_ref, qg_ref, qd_ref, kg_ref, vg_ref, kd_ref, vd_ref = (next(it) for _ in range(7))

    j = pl.program_id(1)
    nt = pl.num_programs(1)
    has_prev = (j > 0).astype(F32)
    has_next = (j < nt - 1).astype(F32)

    def ext(c0):
        return jnp.concatenate([prev_ref[:, c0:c0 + 256] * has_prev,
                                cur_ref[:, c0:c0 + 256],
                                next_ref[:, c0:c0 + 256] * has_next], axis=0)

    ue = ext(C_POOL)
    n = ts + 2 * HALO
    up = lambda x, s: pltpu.roll(x, n - s, 0)
    dn = lambda x, s: pltpu.roll(x, s, 0)
    c2 = ue + dn(ue, 1)
    c4 = up(c2, 1) + dn(c2, 1)
    c8 = up(c4, 2) + dn(c4, 2)
    c16 = up(c8, 4) + dn(c8, 4)
    lane = lax.broadcasted_iota(jnp.int32, (1, POOL_WIDTH), 1)
    mid = slice(HALO, HALO + ts)
    total = jnp.where(lane < 64, c2[mid],
                      jnp.where(lane < 128, c4[mid], jnp.where(lane < 192, c8[mid], c16[mid])))
    half = jnp.where(lane < 64, 1, jnp.where(lane < 128, 2, jnp.where(lane < 192, 4, 8)))
    pos = j * ts + lax.broadcasted_iota(jnp.int32, (ts, 1), 0)
    cnt = jnp.minimum(pos + half, seq_len) - jnp.maximum(pos - half, 0)
    u = cur_ref[:, C_POOL:C_POOL + 256]
    p = total / cnt.astype(F32) - u
    y_pool = _dot(p.astype(BF16), pw_ref[0]) * ps_ref[0]

    ce = ext(C_CONVC) * ext(C_CONVH)
    cw = cw_ref[0]
    conv = (dn(ce, 1)[mid] * cw[0:1] + ce[mid] * cw[1:2] + up(ce, 1)[mid] * cw[2:3])
    y_conv = cur_ref[:, C_CONVB:C_CONVB + 256] * conv
    ypc_ref[...] = jnp.concatenate([y_pool, y_conv], axis=1).astype(BF16)

    q = cur_ref[:, C_QG:C_QG + 256]
    k = cur_ref[:, C_KG:C_KG + 128]
    qw = q * qn_ref[0]
    kw = k * kn_ref[0]
    if rope:
        qw = _rope_tm(qw, cg_ref[...], sgu_ref[...], sgd_ref[...], HEAD_W // 4)
        kw = _rope_tm(kw, cg_ref[...], sgu_ref[...], sgd_ref[...], HEAD_W // 4)
    heads = lambda x, n: [x[:, h * HEAD_W:(h + 1) * HEAD_W] for h in range(n)]
    q_scale = _per_head([_head_rms_scale(x) for x in heads(q, GQA_HEADS)], 256)
    k_scale = _per_head([_head_rms_scale(x) for x in heads(k, GQA_KV_HEADS)], 128)
    qg_ref[...] = (qw * (q_scale * (HEAD_W ** -0.5 * LOG2E))).T.astype(BF16)
    kn = kw * k_scale
    v = cur_ref[:, C_VG:C_VG + 128]

    qd = cur_ref[:, C_QD:C_QD + 256]
    kd = cur_ref[:, C_KD:C_KD + 256]
    vd = cur_ref[:, C_VD:C_VD + 256]
    if rope:
        qd = _rope_tm(qd, cd_ref[...], sdu_ref[...], sdd_ref[...], DIFF_HEAD_DIM // 4)
        kd = _rope_tm(kd, cd_ref[...], sdu_ref[...], sdd_ref[...], DIFF_HEAD_DIM // 4)
    qd_ref[...] = (qd * (DIFF_HEAD_DIM ** -0.5 * LOG2E)).T.astype(BF16)

    if kv_f32:
        for h, (kh, vh) in enumerate(zip(heads(kn, GQA_KV_HEADS), heads(v, GQA_KV_HEADS))):
            kg_ref[0, h] = kh
            vg_ref[0, h] = vh
        for h, (kh, vh) in enumerate(zip(heads(kd, DIFF_HEADS), heads(vd, DIFF_HEADS))):
            kd_ref[0, h] = kh
            vd_ref[0, h] = vh
    else:
        kg_ref[...] = kn.astype(BF16)
        kd_ref[...] = kd.astype(BF16)
        ones = jnp.ones((ts, HEAD_W), F32)
        for h, vh in enumerate(heads(v, GQA_KV_HEADS)):
            vg_ref[2 * HEAD_W * h:2 * HEAD_W * (h + 1), :] = jnp.concatenate([vh, ones], axis=1).T.astype(BF16)
        for h, vh in enumerate(heads(vd, DIFF_HEADS)):
            vd_ref[2 * HEAD_W * h:2 * HEAD_W * (h + 1), :] = jnp.concatenate([vh, ones], axis=1).T.astype(BF16)


def _seqmix(proj, pool_w_bd, pool_scale, conv_w, qn, kn, ropes, *, layer, batch, seq_len, kv_f32):
    T = proj.shape[0]
    ts = _pick(seq_len, (512, 256, 128))
    nt = seq_len // ts
    hb = ts // HALO
    rope = ropes is not None
    tile = lambda b, j: b * nt + j
    in_specs = [
        pl.BlockSpec((HALO, 1024), lambda b, j: (jnp.maximum(tile(b, j) * hb - 1, 0), 0)),
        pl.BlockSpec((ts, MIX_IN), lambda b, j: (tile(b, j), 0)),
        pl.BlockSpec((HALO, 1024), lambda b, j: (jnp.minimum((tile(b, j) + 1) * hb, T // HALO - 1), 0)),
        pl.BlockSpec((1, 256, 256), lambda b, j: (layer, 0, 0)),
        pl.BlockSpec((1, 1, 256), lambda b, j: (layer, 0, 0)),
        pl.BlockSpec((1, 3, 256), lambda b, j: (layer, 0, 0)),
        pl.BlockSpec((1, 1, 256), lambda b, j: (layer, 0, 0)),
        pl.BlockSpec((1, 1, 128), lambda b, j: (layer, 0, 0)),
    ]
    args = [proj, proj, proj, pool_w_bd, pool_scale, conv_w, qn, kn]
    if rope:
        in_specs += [pl.BlockSpec((ts, 128), lambda b, j: (j, 0))] * 6
        args += list(ropes)
    rows = lambda w: pl.BlockSpec((ts, w), lambda b, j: (tile(b, j), 0))
    cols = lambda h: pl.BlockSpec((h, ts), lambda b, j: (0, tile(b, j)))
    out_specs = [rows(512), cols(256), cols(256)]
    out_shape = [jax.ShapeDtypeStruct((T, 512), BF16),
                 jax.ShapeDtypeStruct((256, T), BF16),
                 jax.ShapeDtypeStruct((256, T), BF16)]
    for h, is_key in ((GQA_KV_HEADS, True), (GQA_KV_HEADS, False), (DIFF_HEADS, True), (DIFF_HEADS, False)):
        if kv_f32:
            out_specs.append(pl.BlockSpec((1, h, ts, HEAD_W), lambda b, j: (b, 0, j, 0)))
            out_shape.append(jax.ShapeDtypeStruct((batch, h, seq_len, HEAD_W), F32))
        elif is_key:
            out_specs.append(rows(h * HEAD_W))
            out_shape.append(jax.ShapeDtypeStruct((T, h * HEAD_W), BF16))
        else:
            out_specs.append(cols(h * 2 * HEAD_W))
            out_shape.append(jax.ShapeDtypeStruct((h * 2 * HEAD_W, T), BF16))
    return pl.pallas_call(
        functools.partial(_seqmix_kernel, ts=ts, seq_len=seq_len, rope=rope, kv_f32=kv_f32),
        grid=(batch, nt),
        in_specs=in_specs,
        out_specs=out_specs,
        out_shape=out_shape,
        compiler_params=_cparams(("parallel", "parallel")),
        name="seqmix",
    )(*args)


def _attend_kernel(*refs, has_ctx, lam_init, tq, n_keys):
    it = iter(refs)
    qg_ref, qd_ref, kg_ref, vg_ref, kd_ref, vd_ref = (next(it) for _ in range(6))
    if has_ctx:
        ckg_ref, cvg_ref, ckd_ref, cvd_ref = (next(it) for _ in range(4))
    lam_ref, sub_ref = next(it), next(it)
    yg_ref, yd_ref = next(it), next(it)
    s_ref, p_ref = next(it), next(it)

    def values_t(v):
        return jnp.concatenate([v, jnp.ones_like(v)], axis=1).T[0:VT_ROWS].astype(BF16)

    n_tot = n_keys + (ckg_ref.shape[3] if has_ctx else 0)
    row = lax.broadcasted_iota(jnp.int32, (HEAD_W, 1), 0)

    def gqa_qt(kh):
        r = 2 * HEAD_W * kh
        return jnp.concatenate([qg_ref[r:r + HEAD_W, :], qg_ref[r + HEAD_W:r + 2 * HEAD_W, :]], axis=1)

    def diff_qt(h):
        qh = qd_ref[HEAD_W * h:HEAD_W * (h + 1), :]
        return jnp.concatenate([jnp.where(row < DIFF_HEAD_DIM, qh, jnp.zeros_like(qh)),
                                jnp.where(row >= DIFF_HEAD_DIM, qh, jnp.zeros_like(qh))], axis=1)

    ctx_g = (ckg_ref, cvg_ref) if has_ctx else (None, None)
    ctx_d = (ckd_ref, cvd_ref) if has_ctx else (None, None)
    units = ([(functools.partial(gqa_qt, kh), kg_ref, vg_ref, *ctx_g, kh) for kh in range(GQA_KV_HEADS)]
             + [(functools.partial(diff_qt, h), kd_ref, vd_ref, *ctx_d, h) for h in range(DIFF_HEADS)])
    m_acc = [None] * len(units)

    def score_steps(u):
        qt_fn, k_ref, _, ck_ref, _, h = units[u]
        qt = qt_fn()

        def put(r0, k, q):
            s = _dot(k, q)
            rows = s.shape[0]
            s_ref[u % 2, r0:r0 + rows, :] = s
            blk = jnp.max(s.reshape(rows // 8, 8, s.shape[1]), axis=0)
            m_acc[u] = blk if m_acc[u] is None else jnp.maximum(m_acc[u], blk)

        if not has_ctx:
            return [lambda: put(0, k_ref[0, h].astype(BF16), qt)]
        zeros = jnp.zeros_like(qt)
        qt128 = jnp.concatenate([qt, zeros] if h % 2 == 0 else [zeros, qt], axis=0)
        lanes = slice(128 * (h // 2), 128 * (h // 2) + 128)
        steps = [functools.partial(lambda r0: put(r0, k_ref[r0:r0 + SCORE_CHUNK, lanes], qt128), r0)
                 for r0 in range(0, n_keys, SCORE_CHUNK)]
        return steps + [lambda: put(n_keys, ck_ref[0, 0, h].astype(BF16), qt)]

    def exp_steps(u):
        m = jnp.max(m_acc[u], axis=0, keepdims=True)

        def block(r0):
            p_ref[u % 2, r0:r0 + EXP_CHUNK, :] = jnp.exp2(s_ref[u % 2, r0:r0 + EXP_CHUNK, :] - m).astype(BF16)

        return [functools.partial(block, r0) for r0 in range(0, n_tot, EXP_CHUNK)]

    def weighted_values(u):
        _, _, v_ref, _, cv_ref, h = units[u]
        if has_ctx:
            o = _dot(v_ref[2 * HEAD_W * h:2 * HEAD_W * h + VT_ROWS, :], p_ref[u % 2, 0:n_keys, :])
            o = o + _dot(values_t(cv_ref[0, 0, h]), p_ref[u % 2, n_keys:n_tot, :])
        else:
            o = _dot(values_t(v_ref[0, h]), p_ref[u % 2])
        return o[0:HEAD_W] / o[HEAD_W:HEAD_W + 1]

    for step in score_steps(0):
        step()
    outs = []
    for u in range(len(units)):
        exps = exp_steps(u)
        nxt = score_steps(u + 1) if u + 1 < len(units) else []
        per = -(-len(exps) // max(len(nxt), 1))
        for i in range(max(len(nxt), 1)):
            if i < len(nxt):
                nxt[i]()
            for e in exps[i * per:(i + 1) * per]:
                e()
        outs.append(weighted_values(u))

    yg = []
    for o in outs[:GQA_KV_HEADS]:
        yg += [o[:, 0:tq], o[:, tq:2 * tq]]
    yg_ref[...] = jnp.concatenate(yg, axis=0).T.astype(BF16)

    lv = lam_ref[0]
    lam = (jnp.exp(jnp.sum(lv[0:1] * lv[1:2], axis=-1, keepdims=True))
           - jnp.exp(jnp.sum(lv[2:3] * lv[3:4], axis=-1, keepdims=True)) + lam_init)
    yd = []
    for o in outs[GQA_KV_HEADS:]:
        o = o[:, 0:tq] - lam * o[:, tq:2 * tq]
        o = o * lax.rsqrt(jnp.mean(o * o, axis=0, keepdims=True) + NORM_EPS) * sub_ref[0]
        yd.append(o * (1.0 - lam_init))
    yd_ref[...] = jnp.concatenate(yd, axis=0).T.astype(BF16)


def _attend(qg, qd, kg, vg, kd, vd, ctx, diff_lambda, diff_subln, *, layer, batch, seq_len, lam_init):
    T = qg.shape[1]
    has_ctx = ctx is not None
    tq = _pick(seq_len, (256, 128))
    nq = seq_len // tq
    qspec = pl.BlockSpec((256, tq), lambda b, i: (0, b * nq + i))
    if has_ctx:
        once = dict(pipeline_mode=pl.Buffered(1))
        kspec = lambda h: pl.BlockSpec((seq_len, h * HEAD_W), lambda b, i: (b, 0), **once)
        vspec = lambda h: pl.BlockSpec((h * 2 * HEAD_W, seq_len), lambda b, i: (0, b), **once)
    else:
        kspec = vspec = lambda h: pl.BlockSpec((1, h, seq_len, HEAD_W), lambda b, i: (b, 0, 0, 0))
    in_specs = [qspec, qspec,
                kspec(GQA_KV_HEADS), vspec(GQA_KV_HEADS), kspec(DIFF_HEADS), vspec(DIFF_HEADS)]
    args = [qg, qd, kg, vg, kd, vd]
    n_tot = seq_len
    if has_ctx:
        n_tot += ctx[0].shape[3]
        for c in ctx:
            _, _, h, p, w = c.shape
            in_specs.append(pl.BlockSpec((1, 1, h, p, w), lambda b, i: (b, layer, 0, 0, 0)))
            args.append(c)
    in_specs += [pl.BlockSpec((1, 4, DIFF_HEAD_DIM), lambda b, i: (layer, 0, 0)),
                 pl.BlockSpec((1, HEAD_W, 1), lambda b, i: (layer, 0, 0))]
    args += [diff_lambda, diff_subln]
    ospec = pl.BlockSpec((tq, 256), lambda b, i: (b * nq + i, 0))
    return pl.pallas_call(
        functools.partial(_attend_kernel, has_ctx=has_ctx, lam_init=lam_init, tq=tq, n_keys=seq_len),
        grid=(batch, nq),
        in_specs=in_specs,
        out_specs=[ospec, ospec],
        out_shape=[jax.ShapeDtypeStruct((T, 256), BF16)] * 2,
        scratch_shapes=[pltpu.VMEM((2, n_tot, 2 * tq), F32), pltpu.VMEM((2, n_tot, 2 * tq), BF16)],
        compiler_params=_cparams(("parallel", "parallel")),
        name="attend",
    )(*args)


def _mixout_kernel(x_ref, mod_ref, ypc_ref, yg_ref, yd_ref, w_ref, o_ref):
    w = w_ref[0]
    y = (_dot(ypc_ref[...], w[0:512]) + _dot(yg_ref[...], w[512:768]) + _dot(yd_ref[...], w[768:1024]))
    o_ref[...] = x_ref[...] + mod_ref[0][5:6] * y


def _mixout(x, mods, ypc, yg, yd, w_mix_out, *, layer, row0, rows_per_cond):
    T, D = x.shape
    tm = _pick(T, (512, 256, 128))
    mrow = lambda i: (layer * COND_ROWS + row0 + (i * tm) // rows_per_cond, 0, 0)
    return pl.pallas_call(
        _mixout_kernel,
        grid=(T // tm,),
        in_specs=[pl.BlockSpec((tm, D), lambda i: (i, 0)),
                  pl.BlockSpec((1, N_MOD, D), mrow),
                  pl.BlockSpec((tm, 512), lambda i: (i, 0)),
                  pl.BlockSpec((tm, 256), lambda i: (i, 0)),
                  pl.BlockSpec((tm, 256), lambda i: (i, 0)),
                  pl.BlockSpec((1, D, D), lambda i: (layer, 0, 0))],
        out_specs=pl.BlockSpec((tm, D), lambda i: (i, 0)),
        out_shape=jax.ShapeDtypeStruct((T, D), F32),
        compiler_params=_cparams(("parallel",)),
        name="mixout",
    )(x, mods, ypc, yg, yd, w_mix_out)


def _final_kernel(x_ref, g_ref, o_ref):
    x = x_ref[...]
    ms = jnp.mean(x * x, axis=-1, keepdims=True)
    o_ref[...] = x * lax.rsqrt(ms + NORM_EPS) * g_ref[...]


def _final_norm(x, g):
    T, D = x.shape
    tm = _pick(T, (1024, 512, 256, 128))
    return pl.pallas_call(
        _final_kernel,
        grid=(T // tm,),
        in_specs=[pl.BlockSpec((tm, D), lambda i: (i, 0)), pl.BlockSpec((1, D), lambda i: (0, 0))],
        out_specs=pl.BlockSpec((tm, D), lambda i: (i, 0)),
        out_shape=jax.ShapeDtypeStruct((T, D), F32),
        compiler_params=_cparams(("parallel",)),
        name="final_norm",
    )(x, g)


def _rope_tables(seq, dim):
    rows = seq // GRID_W
    row = jnp.broadcast_to(jnp.arange(rows)[:, None], (rows, GRID_W)).reshape(seq).astype(F32)
    col = jnp.broadcast_to(jnp.arange(GRID_W)[None, :], (rows, GRID_W)).reshape(seq).astype(F32)
    quarter = dim // 4
    inv = 1.0 / (ROPE_THETA ** (jnp.arange(quarter, dtype=F32) / quarter))
    ar = row[:, None] * inv
    ac = col[:, None] * inv
    ang = jnp.tile(jnp.concatenate([ar, ar, ac, ac], axis=-1), (1, 128 // dim))
    chunk = (jnp.arange(128) // quarter) % 2
    cos, sin = jnp.cos(ang), jnp.sin(ang)
    return cos, jnp.where(chunk == 0, -sin, 0.0), jnp.where(chunk == 1, sin, 0.0)


def kernel(x_prompt, x_sample, cache_gqa_k, cache_gqa_v, cache_diff_k, cache_diff_v, c, c_ctx, w_ada, b_ada, norm_ffn1, norm_mix, norm_ffn2, w_ffn1_in, w_ffn1_out, w_ffn2_in, w_ffn2_out, w_mix_in, pool_w, pool_scale, conv_w, gqa_q_norm, gqa_k_norm, diff_lambda, diff_subln, w_mix_out, final_norm):
    bp, sp, D = x_prompt.shape
    bs, ss, _ = x_sample.shape
    L = w_ada.shape[0]
    assert 1 + bs <= COND_ROWS and ss % GRID_W == 0

    cond = jnp.zeros((COND_ROWS, D), F32).at[0].set(c_ctx).at[1:1 + bs].set(c)
    mods = _ada(cond, w_ada, b_ada[:, None, :]).reshape(L * COND_ROWS, N_MOD, D)

    w1i, w1o = w_ffn1_in.astype(BF16), w_ffn1_out.astype(BF16)
    w2i, w2o = w_ffn2_in.astype(BF16), w_ffn2_out.astype(BF16)
    wmi, wmo = w_mix_in.astype(BF16), w_mix_out.astype(BF16)
    g = pool_w.shape[1]
    pw_bd = jnp.einsum('lgcd,gh->lgchd', pool_w, jnp.eye(g, dtype=F32)).reshape(L, POOL_WIDTH, POOL_WIDTH).astype(BF16)
    n1, nm, n2 = norm_ffn1[:, None, :], norm_mix[:, None, :], norm_ffn2[:, None, :]
    ps = pool_scale[:, None, :]
    qn = jnp.tile(gqa_q_norm, (1, GQA_HEADS))[:, None, :]
    kn = jnp.tile(gqa_k_norm, (1, GQA_KV_HEADS))[:, None, :]
    sub = diff_subln[:, :, None]
    ropes = _rope_tables(ss, HEAD_W) + _rope_tables(ss, DIFF_HEAD_DIM)
    ctx = (cache_gqa_k, cache_gqa_v, cache_diff_k, cache_diff_v)
    lam_inits = [0.8 - 0.6 * math.exp(-0.3 * l) for l in range(L)]

    def run_stream(x, batch, seq_len, row0, rows_per_cond, is_ctx_stream):
        kvs = []
        for l in range(L):
            cond_kw = dict(layer=l, row0=row0, rows_per_cond=rows_per_cond)
            x = _ffn(x, mods, n1, w1i, w1o, mod_base=0, **cond_kw)
            proj = _mixin(x, mods, nm, wmi, **cond_kw)
            ypc, qg, qd, kg, vg, kd, vd = _seqmix(
                proj, pw_bd, ps, conv_w, qn, kn, None if is_ctx_stream else ropes,
                layer=l, batch=batch, seq_len=seq_len, kv_f32=is_ctx_stream)
            yg, yd = _attend(qg, qd, kg, vg, kd, vd, None if is_ctx_stream else ctx, diff_lambda, sub,
                             layer=l, batch=batch, seq_len=seq_len, lam_init=lam_inits[l])
            x = _mixout(x, mods, ypc, yg, yd, wmo, **cond_kw)
            x = _ffn(x, mods, n2, w2i, w2o, mod_base=6, **cond_kw)
            kvs.append((kg, vg, kd, vd))
        return _final_norm(x, final_norm[None, :]), kvs

    yp, kvs = run_stream(x_prompt.reshape(bp * sp, D), bp, sp, 0, bp * sp, True)
    ys, _ = run_stream(x_sample.reshape(bs * ss, D), bs, ss, 1, ss, False)
    new_kv = [jnp.stack([kv[i] for kv in kvs], axis=1) for i in range(4)]
    return (yp.reshape(bp, sp, D), ys.reshape(bs, ss, D), *new_kv)
```

```python
import functools
import math

import jax
import jax.numpy as jnp
from jax import lax
from jax.experimental import pallas as pl
from jax.experimental.pallas import tpu as pltpu

F32 = jnp.float32
BF16 = jnp.bfloat16

N_MOD = 9
NORM_EPS = 1e-6
ROPE_THETA = 10000.0
GRID_W = 64
POOL_WINDOWS = (2, 4, 8, 16)
POOL_WIDTH = 256
CONV_WIDTH = 256
GQA_HEADS = 4
GQA_KV_HEADS = 2
GQA_GROUP = GQA_HEADS // GQA_KV_HEADS
HEAD_W = 64
DIFF_HEADS = 4
DIFF_HEAD_DIM = 32
HALO = 8
COND_ROWS = 16
LOG2E = math.log2(math.e)
SCORE_CHUNK = 512
EXP_CHUNK = 128
FFN_TM = 512
FFN_CHUNK = 256

C_POOL, C_CONVH, C_CONVB, C_CONVC = 0, 256, 512, 768
C_QG, C_KG, C_VG, C_QD, C_KD, C_VD = 1024, 1280, 1408, 1536, 1792, 2048
MIX_IN = 2304

VMEM_LIMIT_BYTES = 56 * 1024 * 1024


def _cparams(sem):
    return pltpu.CompilerParams(dimension_semantics=sem, vmem_limit_bytes=VMEM_LIMIT_BYTES)


def _pick(n, cands):
    for c in cands:
        if n % c == 0:
            return c
    raise ValueError(f"no tile in {cands} divides {n}")


def _dot(a, b):
    return jnp.dot(a, b, preferred_element_type=F32)


def _dot_nt(a, b):
    return lax.dot_general(a, b, (((1,), (1,)), ((), ())), preferred_element_type=F32)


def _norm_mod(x, g, shift, scale):
    ms = jnp.mean(x * x, axis=-1, keepdims=True)
    return (x * lax.rsqrt(ms + NORM_EPS) * g) * (1.0 + scale) + shift


def _ada_kernel(c_ref, w_ref, b_ref, o_ref):
    c = c_ref[...]
    a = (c * jax.nn.sigmoid(c)).astype(BF16)
    o_ref[0] = _dot(a, w_ref[0].astype(BF16)) + b_ref[0]


def _ada(cond, w_ada, b_ada):
    L, D, N = w_ada.shape
    R = cond.shape[0]
    tn = _pick(N, (1024, 512, 256, 128))
    return pl.pallas_call(
        _ada_kernel,
        grid=(L, N // tn),
        in_specs=[pl.BlockSpec((R, D), lambda l, j: (0, 0)),
                  pl.BlockSpec((1, D, tn), lambda l, j: (l, 0, j)),
                  pl.BlockSpec((1, 1, tn), lambda l, j: (l, 0, j))],
        out_specs=pl.BlockSpec((1, R, tn), lambda l, j: (l, 0, j)),
        out_shape=jax.ShapeDtypeStruct((L, R, N), F32),
        compiler_params=_cparams(("parallel", "parallel")),
        name="ada",
    )(cond, w_ada, b_ada)


def _ffn_kernel(x_ref, mod_ref, g_ref, wi_ref, wo_ref, o_ref, a_ref, *, mod_base, chunk):
    m = mod_ref[0]
    x = x_ref[...]
    h = _norm_mod(x, g_ref[0], m[mod_base:mod_base + 1], m[mod_base + 1:mod_base + 2]).astype(BF16)
    ffn_dim = wo_ref.shape[1]
    for c0 in range(0, ffn_dim, chunk):
        g = _dot(h, wi_ref[0, :, c0:c0 + chunk])
        u = _dot(h, wi_ref[0, :, ffn_dim + c0:ffn_dim + c0 + chunk])
        a_ref[:, c0:c0 + chunk] = (g * jax.nn.sigmoid(g) * u).astype(BF16)
    y = _dot(a_ref[...], wo_ref[0])
    o_ref[...] = x + (0.5 * m[mod_base + 2:mod_base + 3]) * y


def _ffn(x, mods, norm_g, w_in, w_out, *, layer, mod_base, row0, rows_per_cond):
    T, D = x.shape
    F = w_out.shape[1]
    tm = _pick(T, (FFN_TM, 256, 128))
    chunk = _pick(F, (FFN_CHUNK, 128))
    mrow = lambda i: (layer * COND_ROWS + row0 + (i * tm) // rows_per_cond, 0, 0)
    resident = dict(pipeline_mode=pl.Buffered(1))
    return pl.pallas_call(
        functools.partial(_ffn_kernel, mod_base=mod_base, chunk=chunk),
        grid=(T // tm,),
        in_specs=[pl.BlockSpec((tm, D), lambda i: (i, 0)),
                  pl.BlockSpec((1, N_MOD, D), mrow),
                  pl.BlockSpec((1, 1, D), lambda i: (layer, 0, 0)),
                  pl.BlockSpec((1, D, 2 * F), lambda i: (layer, 0, 0), **resident),
                  pl.BlockSpec((1, F, D), lambda i: (layer, 0, 0), **resident)],
        out_specs=pl.BlockSpec((tm, D), lambda i: (i, 0)),
        out_shape=jax.ShapeDtypeStruct((T, D), F32),
        scratch_shapes=[pltpu.VMEM((tm, F), BF16)],
        compiler_params=_cparams(("parallel",)),
        name="ffn",
    )(x, mods, norm_g, w_in, w_out)


def _rope_tm(x, cos, sin_up, sin_dn, quarter):
    outs = []
    for c0 in range(0, x.shape[1], 128):
        xb = x[:, c0:c0 + 128]
        up = pltpu.roll(xb, 128 - quarter, 1)
        dn = pltpu.roll(xb, quarter, 1)
        outs.append(xb * cos + up * sin_up + dn * sin_dn)
    return outs[0] if len(outs) == 1 else jnp.concatenate(outs, axis=1)


def _head_rms_scale(x):
    return lax.rsqrt(jnp.mean(x * x, axis=-1, keepdims=True) + NORM_EPS)


def _per_head(scales, width):
    lane = lax.broadcasted_iota(jnp.int32, (1, width), 1)
    out = scales[-1]
    for h in range(len(scales) - 2, -1, -1):
        out = jnp.where(lane < (h + 1) * HEAD_W, scales[h], out)
    return out


def _seqmix_kernel(*refs, ts, seq_len, rope, kv_f32):
    it = iter(refs)
    xp_ref, xc_ref, xn_ref, mod_ref, g_ref, w_ref = (next(it) for _ in range(6))
    pw_ref, ps_ref, cw_ref, qn_ref, kn_ref = next(it), next(it), next(it), next(it), next(it)
    if rope:
        cg_ref, sgu_ref, sgd_ref, cd_ref, sdu_ref, sdd_ref = (next(it) for _ in range(6))
    ypc_ref, qg_ref, qd_ref, kg_ref, vg_ref, kd_ref, vd_ref = (next(it) for _ in range(7))

    j = pl.program_id(1)
    nt = pl.num_programs(1)
    n = ts + 2 * HALO
    mid = slice(HALO, HALO + ts)

    m = mod_ref[0]
    xe = jnp.concatenate([xp_ref[...], xc_ref[...], xn_ref[...]], axis=0)
    he = _norm_mod(xe, g_ref[0], m[3:4], m[4:5])
    w = w_ref[0]
    i = lax.broadcasted_iota(jnp.int32, (n, 1), 0)
    has_prev = (j > 0).astype(F32)
    has_next = (j < nt - 1).astype(F32)
    in_seq = jnp.where(i < HALO, has_prev, jnp.where(i >= HALO + ts, has_next, 1.0))
    proj_ext = _dot(he.astype(BF16), w[:, 0:C_QG]) * in_seq
    proj_att = _dot(he[mid].astype(BF16), w[:, C_QG:MIX_IN])

    def ext(c0):
        return proj_ext[:, c0:c0 + 256]

    def cur(c0, width):
        if c0 < C_QG:
            return proj_ext[mid, c0:c0 + width]
        return proj_att[:, c0 - C_QG:c0 - C_QG + width]

    ue = ext(C_POOL)
    up = lambda x, s: pltpu.roll(x, n - s, 0)
    dn = lambda x, s: pltpu.roll(x, s, 0)
    c2 = ue + dn(ue, 1)
    c4 = up(c2, 1) + dn(c2, 1)
    c8 = up(c4, 2) + dn(c4, 2)
    c16 = up(c8, 4) + dn(c8, 4)
    lane = lax.broadcasted_iota(jnp.int32, (1, POOL_WIDTH), 1)
    total = jnp.where(lane < 64, c2[mid],
                      jnp.where(lane < 128, c4[mid], jnp.where(lane < 192, c8[mid], c16[mid])))
    half = jnp.where(lane < 64, 1, jnp.where(lane < 128, 2, jnp.where(lane < 192, 4, 8)))
    pos = j * ts + lax.broadcasted_iota(jnp.int32, (ts, 1), 0)
    cnt = jnp.minimum(pos + half, seq_len) - jnp.maximum(pos - half, 0)
    u = cur(C_POOL, 256)
    p = total / cnt.astype(F32) - u
    y_pool = _dot(p.astype(BF16), pw_ref[0]) * ps_ref[0]

    ce = ext(C_CONVC) * ext(C_CONVH)
    cw = cw_ref[0]
    conv = (dn(ce, 1)[mid] * cw[0:1] + ce[mid] * cw[1:2] + up(ce, 1)[mid] * cw[2:3])
    y_conv = cur(C_CONVB, 256) * conv
    ypc_ref[...] = jnp.concatenate([y_pool, y_conv], axis=1).astype(BF16)

    q = cur(C_QG, 256)
    k = cur(C_KG, 128)
    qw = q * qn_ref[0]
    kw = k * kn_ref[0]
    if rope:
        qw = _rope_tm(qw, cg_ref[...], sgu_ref[...], sgd_ref[...], HEAD_W // 4)
        kw = _rope_tm(kw, cg_ref[...], sgu_ref[...], sgd_ref[...], HEAD_W // 4)
    heads = lambda x, n: [x[:, h * HEAD_W:(h + 1) * HEAD_W] for h in range(n)]
    q_scale = _per_head([_head_rms_scale(x) for x in heads(q, GQA_HEADS)], 256)
    k_scale = _per_head([_head_rms_scale(x) for x in heads(k, GQA_KV_HEADS)], 128)
    qg_ref[...] = (qw * (q_scale * (HEAD_W ** -0.5 * LOG2E))).astype(BF16)
    kn = kw * k_scale
    v = cur(C_VG, 128)

    qd = cur(C_QD, 256)
    kd = cur(C_KD, 256)
    vd = cur(C_VD, 256)
    if rope:
        qd = _rope_tm(qd, cd_ref[...], sdu_ref[...], sdd_ref[...], DIFF_HEAD_DIM // 4)
        kd = _rope_tm(kd, cd_ref[...], sdu_ref[...], sdd_ref[...], DIFF_HEAD_DIM // 4)
    qd_ref[...] = (qd * (DIFF_HEAD_DIM ** -0.5 * LOG2E)).astype(BF16)

    if kv_f32:
        for h, (kh, vh) in enumerate(zip(heads(kn, GQA_KV_HEADS), heads(v, GQA_KV_HEADS))):
            kg_ref[0, h] = kh
            vg_ref[0, h] = vh
        for h, (kh, vh) in enumerate(zip(heads(kd, DIFF_HEADS), heads(vd, DIFF_HEADS))):
            kd_ref[0, h] = kh
            vd_ref[0, h] = vh
    else:
        kg_ref[...] = kn.T.astype(BF16)
        kd_ref[...] = kd.T.astype(BF16)
        ones = jnp.ones((ts, HEAD_W), F32)
        for h, vh in enumerate(heads(v, GQA_KV_HEADS)):
            vg_ref[h] = jnp.concatenate([vh, ones], axis=1).astype(BF16)
        for h, vh in enumerate(heads(vd, DIFF_HEADS)):
            vd_ref[h] = jnp.concatenate([vh, ones], axis=1).astype(BF16)


def _seqmix(x, mods, norm_g, w_mix_in, pool_w_bd, pool_scale, conv_w, qn, kn, ropes, *,
            layer, batch, seq_len, kv_f32, row0, rows_per_cond):
    T, D = x.shape
    ts = _pick(seq_len, (512, 256, 128))
    nt = seq_len // ts
    hb = ts // HALO
    rope = ropes is not None
    tile = lambda b, j: b * nt + j
    in_specs = [
        pl.BlockSpec((HALO, D), lambda b, j: (jnp.maximum(tile(b, j) * hb - 1, 0), 0)),
        pl.BlockSpec((ts, D), lambda b, j: (tile(b, j), 0)),
        pl.BlockSpec((HALO, D), lambda b, j: (jnp.minimum((tile(b, j) + 1) * hb, T // HALO - 1), 0)),
        pl.BlockSpec((1, N_MOD, D), lambda b, j: (layer * COND_ROWS + row0 + (b * seq_len) // rows_per_cond, 0, 0)),
        pl.BlockSpec((1, 1, D), lambda b, j: (layer, 0, 0)),
        pl.BlockSpec((1, D, MIX_IN), lambda b, j: (layer, 0, 0), pipeline_mode=pl.Buffered(1)),
        pl.BlockSpec((1, 256, 256), lambda b, j: (layer, 0, 0)),
        pl.BlockSpec((1, 1, 256), lambda b, j: (layer, 0, 0)),
        pl.BlockSpec((1, 3, 256), lambda b, j: (layer, 0, 0)),
        pl.BlockSpec((1, 1, 256), lambda b, j: (layer, 0, 0)),
        pl.BlockSpec((1, 1, 128), lambda b, j: (layer, 0, 0)),
    ]
    args = [x, x, x, mods, norm_g, w_mix_in, pool_w_bd, pool_scale, conv_w, qn, kn]
    if rope:
        in_specs += [pl.BlockSpec((ts, 128), lambda b, j: (j, 0))] * 6
        args += list(ropes)
    rows = lambda w: pl.BlockSpec((ts, w), lambda b, j: (tile(b, j), 0))
    cols = lambda h: pl.BlockSpec((h, ts), lambda b, j: (0, tile(b, j)))
    out_specs = [rows(512), rows(256), rows(256)]
    out_shape = [jax.ShapeDtypeStruct((T, 512), BF16),
                 jax.ShapeDtypeStruct((T, 256), BF16),
                 jax.ShapeDtypeStruct((T, 256), BF16)]
    for h, is_key in ((GQA_KV_HEADS, True), (GQA_KV_HEADS, False), (DIFF_HEADS, True), (DIFF_HEADS, False)):
        if kv_f32:
            out_specs.append(pl.BlockSpec((1, h, ts, HEAD_W), lambda b, j: (b, 0, j, 0)))
            out_shape.append(jax.ShapeDtypeStruct((batch, h, seq_len, HEAD_W), F32))
        elif is_key:
            out_specs.append(cols(h * HEAD_W))
            out_shape.append(jax.ShapeDtypeStruct((h * HEAD_W, T), BF16))
        else:
            out_specs.append(pl.BlockSpec((h, ts, 2 * HEAD_W), lambda b, j: (0, tile(b, j), 0)))
            out_shape.append(jax.ShapeDtypeStruct((h, T, 2 * HEAD_W), BF16))
    return pl.pallas_call(
        functools.partial(_seqmix_kernel, ts=ts, seq_len=seq_len, rope=rope, kv_f32=kv_f32),
        grid=(batch, nt),
        in_specs=in_specs,
        out_specs=out_specs,
        out_shape=out_shape,
        compiler_params=_cparams(("parallel", "parallel")),
        name="seqmix",
    )(*args)


def _attend_kernel(*refs, has_ctx, lam_init, tq, n_keys):
    it = iter(refs)
    qg_ref, qd_ref, kg_ref, vg_ref, kd_ref, vd_ref = (next(it) for _ in range(6))
    if has_ctx:
        ckg_ref, cvg_ref, ckd_ref, cvd_ref = (next(it) for _ in range(4))
    lam_ref, sub_ref = next(it), next(it)
    yg_ref, yd_ref = next(it), next(it)
    s_ref, p_ref = next(it), next(it)

    def with_ones(v):
        return jnp.concatenate([v, jnp.ones_like(v)], axis=1).astype(BF16)

    n_tot = n_keys + (ckg_ref.shape[3] if has_ctx else 0)
    lane = lax.broadcasted_iota(jnp.int32, (1, HEAD_W), 1)

    def gqa_q(kh):
        c = 2 * HEAD_W * kh
        return jnp.concatenate([qg_ref[:, c:c + HEAD_W], qg_ref[:, c + HEAD_W:c + 2 * HEAD_W]], axis=0)

    def diff_q(h):
        qh = qd_ref[:, HEAD_W * h:HEAD_W * (h + 1)]
        return jnp.concatenate([jnp.where(lane < DIFF_HEAD_DIM, qh, jnp.zeros_like(qh)),
                                jnp.where(lane >= DIFF_HEAD_DIM, qh, jnp.zeros_like(qh))], axis=0)

    ctx_g = (ckg_ref, cvg_ref) if has_ctx else (None, None)
    ctx_d = (ckd_ref, cvd_ref) if has_ctx else (None, None)
    units = ([(functools.partial(gqa_q, kh), kg_ref, vg_ref, *ctx_g, kh) for kh in range(GQA_KV_HEADS)]
             + [(functools.partial(diff_q, h), kd_ref, vd_ref, *ctx_d, h) for h in range(DIFF_HEADS)])
    m_acc = [None] * len(units)

    def score_steps(u):
        q_fn, k_ref, _, ck_ref, _, h = units[u]
        q = q_fn()

        def put(c0, s):
            s_ref[u % 2, :, c0:c0 + s.shape[1]] = s
            for b0 in range(0, s.shape[1], 128):
                blk = s[:, b0:b0 + 128]
                m_acc[u] = blk if m_acc[u] is None else jnp.maximum(m_acc[u], blk)

        if not has_ctx:
            return [lambda: put(0, _dot_nt(q, k_ref[0, h].astype(BF16)))]
        rows = slice(h * HEAD_W, (h + 1) * HEAD_W)
        steps = [functools.partial(lambda c0: put(c0, _dot(q, k_ref[rows, c0:c0 + SCORE_CHUNK])), c0)
                 for c0 in range(0, n_keys, SCORE_CHUNK)]
        return steps + [lambda: put(n_keys, _dot_nt(q, ck_ref[0, 0, h].astype(BF16)))]

    def exp_steps(u):
        m = jnp.max(m_acc[u], axis=-1, keepdims=True)

        def block(c0):
            p_ref[u % 2, :, c0:c0 + EXP_CHUNK] = jnp.exp2(s_ref[u % 2, :, c0:c0 + EXP_CHUNK] - m).astype(BF16)

        return [functools.partial(block, c0) for c0 in range(0, n_tot, EXP_CHUNK)]

    def weighted_values(u):
        _, _, v_ref, _, cv_ref, h = units[u]
        if has_ctx:
            o = _dot(p_ref[u % 2, :, 0:n_keys], v_ref[h])
            o = o + _dot(p_ref[u % 2, :, n_keys:n_tot], with_ones(cv_ref[0, 0, h]))
        else:
            o = _dot(p_ref[u % 2], with_ones(v_ref[0, h]))
        return o[:, 0:HEAD_W] / o[:, HEAD_W:HEAD_W + 1]

    for step in score_steps(0):
        step()
    outs = []
    for u in range(len(units)):
        exps = exp_steps(u)
        nxt = score_steps(u + 1) if u + 1 < len(units) else []
        per = -(-len(exps) // max(len(nxt), 1))
        for i in range(max(len(nxt), 1)):
            if i < len(nxt):
                nxt[i]()
            for e in exps[i * per:(i + 1) * per]:
                e()
        outs.append(weighted_values(u))

    yg = []
    for o in outs[:GQA_KV_HEADS]:
        yg += [o[0:tq], o[tq:2 * tq]]
    yg_ref[...] = jnp.concatenate(yg, axis=1).astype(BF16)

    lv = lam_ref[0]
    lam = (jnp.exp(jnp.sum(lv[0:1] * lv[1:2], axis=-1, keepdims=True))
           - jnp.exp(jnp.sum(lv[2:3] * lv[3:4], axis=-1, keepdims=True)) + lam_init)
    yd = []
    for o in outs[GQA_KV_HEADS:]:
        o = o[0:tq] - lam * o[tq:2 * tq]
        o = o * _head_rms_scale(o) * sub_ref[0]
        yd.append(o * (1.0 - lam_init))
    yd_ref[...] = jnp.concatenate(yd, axis=1).astype(BF16)


def _attend(qg, qd, kg, vg, kd, vd, ctx, diff_lambda, diff_subln, *, layer, batch, seq_len, lam_init):
    T = qg.shape[0]
    has_ctx = ctx is not None
    tq = _pick(seq_len, (256, 128))
    nq = seq_len // tq
    qspec = pl.BlockSpec((tq, 256), lambda b, i: (b * nq + i, 0))
    if has_ctx:
        once = dict(pipeline_mode=pl.Buffered(1))
        kspec = lambda h: pl.BlockSpec((h * HEAD_W, seq_len), lambda b, i: (0, b), **once)
        vspec = lambda h: pl.BlockSpec((h, seq_len, 2 * HEAD_W), lambda b, i: (0, b, 0), **once)
    else:
        kspec = vspec = lambda h: pl.BlockSpec((1, h, seq_len, HEAD_W), lambda b, i: (b, 0, 0, 0))
    in_specs = [qspec, qspec,
                kspec(GQA_KV_HEADS), vspec(GQA_KV_HEADS), kspec(DIFF_HEADS), vspec(DIFF_HEADS)]
    args = [qg, qd, kg, vg, kd, vd]
    n_tot = seq_len
    if has_ctx:
        n_tot += ctx[0].shape[3]
        for c in ctx:
            _, _, h, p, w = c.shape
            in_specs.append(pl.BlockSpec((1, 1, h, p, w), lambda b, i: (b, layer, 0, 0, 0)))
            args.append(c)
    in_specs += [pl.BlockSpec((1, 4, DIFF_HEAD_DIM), lambda b, i: (layer, 0, 0)),
                 pl.BlockSpec((1, 1, HEAD_W), lambda b, i: (layer, 0, 0))]
    args += [diff_lambda, diff_subln]
    ospec = pl.BlockSpec((tq, 256), lambda b, i: (b * nq + i, 0))
    return pl.pallas_call(
        functools.partial(_attend_kernel, has_ctx=has_ctx, lam_init=lam_init, tq=tq, n_keys=seq_len),
        grid=(batch, nq),
        in_specs=in_specs,
        out_specs=[ospec, ospec],
        out_shape=[jax.ShapeDtypeStruct((T, 256), BF16)] * 2,
        scratch_shapes=[pltpu.VMEM((2, 2 * tq, n_tot), F32), pltpu.VMEM((2, 2 * tq, n_tot), BF16)],
        compiler_params=_cparams(("parallel", "parallel")),
        name="attend",
    )(*args)


def _mixout_kernel(x_ref, mod_ref, ypc_ref, yg_ref, yd_ref, w_ref, o_ref):
    w = w_ref[0]
    y = (_dot(ypc_ref[...], w[0:512]) + _dot(yg_ref[...], w[512:768]) + _dot(yd_ref[...], w[768:1024]))
    o_ref[...] = x_ref[...] + mod_ref[0][5:6] * y


def _mixout(x, mods, ypc, yg, yd, w_mix_out, *, layer, row0, rows_per_cond):
    T, D = x.shape
    tm = _pick(T, (512, 256, 128))
    mrow = lambda i: (layer * COND_ROWS + row0 + (i * tm) // rows_per_cond, 0, 0)
    return pl.pallas_call(
        _mixout_kernel,
        grid=(T // tm,),
        in_specs=[pl.BlockSpec((tm, D), lambda i: (i, 0)),
                  pl.BlockSpec((1, N_MOD, D), mrow),
                  pl.BlockSpec((tm, 512), lambda i: (i, 0)),
                  pl.BlockSpec((tm, 256), lambda i: (i, 0)),
                  pl.BlockSpec((tm, 256), lambda i: (i, 0)),
                  pl.BlockSpec((1, D, D), lambda i: (layer, 0, 0))],
        out_specs=pl.BlockSpec((tm, D), lambda i: (i, 0)),
        out_shape=jax.ShapeDtypeStruct((T, D), F32),
        compiler_params=_cparams(("parallel",)),
        name="mixout",
    )(x, mods, ypc, yg, yd, w_mix_out)


def _final_kernel(x_ref, g_ref, o_ref):
    x = x_ref[...]
    ms = jnp.mean(x * x, axis=-1, keepdims=True)
    o_ref[...] = x * lax.rsqrt(ms + NORM_EPS) * g_ref[...]


def _final_norm(x, g):
    T, D = x.shape
    tm = _pick(T, (1024, 512, 256, 128))
    return pl.pallas_call(
        _final_kernel,
        grid=(T // tm,),
        in_specs=[pl.BlockSpec((tm, D), lambda i: (i, 0)), pl.BlockSpec((1, D), lambda i: (0, 0))],
        out_specs=pl.BlockSpec((tm, D), lambda i: (i, 0)),
        out_shape=jax.ShapeDtypeStruct((T, D), F32),
        compiler_params=_cparams(("parallel",)),
        name="final_norm",
    )(x, g)


def _rope_tables(seq, dim):
    rows = seq // GRID_W
    row = jnp.broadcast_to(jnp.arange(rows)[:, None], (rows, GRID_W)).reshape(seq).astype(F32)
    col = jnp.broadcast_to(jnp.arange(GRID_W)[None, :], (rows, GRID_W)).reshape(seq).astype(F32)
    quarter = dim // 4
    inv = 1.0 / (ROPE_THETA ** (jnp.arange(quarter, dtype=F32) / quarter))
    ar = row[:, None] * inv
    ac = col[:, None] * inv
    ang = jnp.tile(jnp.concatenate([ar, ar, ac, ac], axis=-1), (1, 128 // dim))
    chunk = (jnp.arange(128) // quarter) % 2
    cos, sin = jnp.cos(ang), jnp.sin(ang)
    return cos, jnp.where(chunk == 0, -sin, 0.0), jnp.where(chunk == 1, sin, 0.0)


def kernel(x_prompt, x_sample, cache_gqa_k, cache_gqa_v, cache_diff_k, cache_diff_v, c, c_ctx, w_ada, b_ada, norm_ffn1, norm_mix, norm_ffn2, w_ffn1_in, w_ffn1_out, w_ffn2_in, w_ffn2_out, w_mix_in, pool_w, pool_scale, conv_w, gqa_q_norm, gqa_k_norm, diff_lambda, diff_subln, w_mix_out, final_norm):
    bp, sp, D = x_prompt.shape
    bs, ss, _ = x_sample.shape
    L = w_ada.shape[0]
    assert 1 + bs <= COND_ROWS and ss % GRID_W == 0

    cond = jnp.zeros((COND_ROWS, D), F32).at[0].set(c_ctx).at[1:1 + bs].set(c)
    mods = _ada(cond, w_ada, b_ada[:, None, :]).reshape(L * COND_ROWS, N_MOD, D)

    w1i, w1o = w_ffn1_in.astype(BF16), w_ffn1_out.astype(BF16)
    w2i, w2o = w_ffn2_in.astype(BF16), w_ffn2_out.astype(BF16)
    wmi, wmo = w_mix_in.astype(BF16), w_mix_out.astype(BF16)
    g = pool_w.shape[1]
    pw_bd = jnp.einsum('lgcd,gh->lgchd', pool_w, jnp.eye(g, dtype=F32)).reshape(L, POOL_WIDTH, POOL_WIDTH).astype(BF16)
    n1, nm, n2 = norm_ffn1[:, None, :], norm_mix[:, None, :], norm_ffn2[:, None, :]
    ps = pool_scale[:, None, :]
    qn = jnp.tile(gqa_q_norm, (1, GQA_HEADS))[:, None, :]
    kn = jnp.tile(gqa_k_norm, (1, GQA_KV_HEADS))[:, None, :]
    sub = diff_subln[:, None, :]
    ropes = _rope_tables(ss, HEAD_W) + _rope_tables(ss, DIFF_HEAD_DIM)
    ctx = (cache_gqa_k, cache_gqa_v, cache_diff_k, cache_diff_v)
    lam_inits = [0.8 - 0.6 * math.exp(-0.3 * l) for l in range(L)]

    def run_stream(x, batch, seq_len, row0, rows_per_cond, is_ctx_stream):
        kvs = []
        for l in range(L):
            cond_kw = dict(layer=l, row0=row0, rows_per_cond=rows_per_cond)
            x = _ffn(x, mods, n1, w1i, w1o, mod_base=0, **cond_kw)
            ypc, qg, qd, kg, vg, kd, vd = _seqmix(
                x, mods, nm, wmi, pw_bd, ps, conv_w, qn, kn, None if is_ctx_stream else ropes,
                batch=batch, seq_len=seq_len, kv_f32=is_ctx_stream, **cond_kw)
            yg, yd = _attend(qg, qd, kg, vg, kd, vd, None if is_ctx_stream else ctx, diff_lambda, sub,
                             layer=l, batch=batch, seq_len=seq_len, lam_init=lam_inits[l])
            x = _mixout(x, mods, ypc, yg, yd, wmo, **cond_kw)
            x = _ffn(x, mods, n2, w2i, w2o, mod_base=6, **cond_kw)
            kvs.append((kg, vg, kd, vd))
        return _final_norm(x, final_norm[None, :]), kvs

    yp, kvs = run_stream(x_prompt.reshape(bp * sp, D), bp, sp, 0, bp * sp, True)
    ys, _ = run_stream(x_sample.reshape(bs * ss, D), bs, ss, 1, ss, False)
    new_kv = [jnp.stack([kv[i] for kv in kvs], axis=1) for i in range(4)]
    return (yp.reshape(bp, sp, D), ys.reshape(bs, ss, D), *new_kv)
```

```python
import functools
import math

import jax
import jax.numpy as jnp
from jax import lax
from jax.experimental import pallas as pl
from jax.experimental.pallas import tpu as pltpu

F32 = jnp.float32
BF16 = jnp.bfloat16

N_MOD = 9
NORM_EPS = 1e-6
ROPE_THETA = 10000.0
GRID_W = 64
POOL_WINDOWS = (2, 4, 8, 16)
POOL_WIDTH = 256
CONV_WIDTH = 256
GQA_HEADS = 4
GQA_KV_HEADS = 2
GQA_GROUP = GQA_HEADS // GQA_KV_HEADS
HEAD_W = 64
DIFF_HEADS = 4
DIFF_HEAD_DIM = 32
HALO = 8
COND_ROWS = 16
LOG2E = math.log2(math.e)
SCORE_CHUNK = 512
EXP_CHUNK = 128
VT_ROWS = 80
ATTEND_QB = 2
FFN_TM = 512
FFN_CHUNK = 256

C_POOL, C_CONVH, C_CONVB, C_CONVC = 0, 256, 512, 768
C_QG, C_KG, C_VG, C_QD, C_KD, C_VD = 1024, 1280, 1408, 1536, 1792, 2048
MIX_IN = 2304

VMEM_LIMIT_BYTES = 56 * 1024 * 1024


def _cparams(sem):
    return pltpu.CompilerParams(dimension_semantics=sem, vmem_limit_bytes=VMEM_LIMIT_BYTES)


def _pick(n, cands):
    for c in cands:
        if n % c == 0:
            return c
    raise ValueError(f"no tile in {cands} divides {n}")


def _dot(a, b):
    return jnp.dot(a, b, preferred_element_type=F32)


def _dot_nt(a, b):
    return lax.dot_general(a, b, (((1,), (1,)), ((), ())), preferred_element_type=F32)


def _norm_mod(x, g, shift, scale):
    ms = jnp.mean(x * x, axis=-1, keepdims=True)
    return (x * lax.rsqrt(ms + NORM_EPS) * g) * (1.0 + scale) + shift


def _ada_kernel(c_ref, w_ref, b_ref, o_ref):
    c = c_ref[...]
    a = (c * jax.nn.sigmoid(c)).astype(BF16)
    o_ref[0] = _dot(a, w_ref[0].astype(BF16)) + b_ref[0]


def _ada(cond, w_ada, b_ada):
    L, D, N = w_ada.shape
    R = cond.shape[0]
    tn = _pick(N, (1024, 512, 256, 128))
    return pl.pallas_call(
        _ada_kernel,
        grid=(L, N // tn),
        in_specs=[pl.BlockSpec((R, D), lambda l, j: (0, 0)),
                  pl.BlockSpec((1, D, tn), lambda l, j: (l, 0, j)),
                  pl.BlockSpec((1, 1, tn), lambda l, j: (l, 0, j))],
        out_specs=pl.BlockSpec((1, R, tn), lambda l, j: (l, 0, j)),
        out_shape=jax.ShapeDtypeStruct((L, R, N), F32),
        compiler_params=_cparams(("parallel", "parallel")),
        name="ada",
    )(cond, w_ada, b_ada)


def _ffn_kernel(*refs, mod_base, chunk, with_mixer):
    it = iter(refs)
    x_ref, mod_ref, g_ref, wi_ref, wo_ref = (next(it) for _ in range(5))
    if with_mixer:
        ypc_ref, yg_ref, yd_ref, wm_ref = (next(it) for _ in range(4))
    o_ref, a_ref = next(it), next(it)
    m = mod_ref[0]
    x = x_ref[...]
    if with_mixer:
        wm = wm_ref[0]
        n_pc, n_g = ypc_ref.shape[1], yg_ref.shape[1]
        y = (_dot(ypc_ref[...], wm[0:n_pc]) + _dot(yg_ref[...], wm[n_pc:n_pc + n_g])
             + _dot(yd_ref[...], wm[n_pc + n_g:]))
        x = x + m[mod_base - 1:mod_base] * y
    h = _norm_mod(x, g_ref[0], m[mod_base:mod_base + 1], m[mod_base + 1:mod_base + 2]).astype(BF16)
    ffn_dim = wo_ref.shape[1]
    for c0 in range(0, ffn_dim, chunk):
        g = _dot(h, wi_ref[0, :, c0:c0 + chunk])
        u = _dot(h, wi_ref[0, :, ffn_dim + c0:ffn_dim + c0 + chunk])
        a_ref[:, c0:c0 + chunk] = (g * jax.nn.sigmoid(g) * u).astype(BF16)
    y = _dot(a_ref[...], wo_ref[0])
    o_ref[...] = x + (0.5 * m[mod_base + 2:mod_base + 3]) * y


def _ffn(x, mods, norm_g, w_in, w_out, mixer=None, *, layer, mod_base, row0, rows_per_cond):
    T, D = x.shape
    F = w_out.shape[1]
    tm = _pick(T, (FFN_TM, 256, 128))
    chunk = _pick(F, (FFN_CHUNK, 128))
    mrow = lambda i: (layer * COND_ROWS + row0 + (i * tm) // rows_per_cond, 0, 0)
    resident = dict(pipeline_mode=pl.Buffered(1))
    in_specs = [pl.BlockSpec((tm, D), lambda i: (i, 0)),
                pl.BlockSpec((1, N_MOD, D), mrow),
                pl.BlockSpec((1, 1, D), lambda i: (layer, 0, 0)),
                pl.BlockSpec((1, D, 2 * F), lambda i: (layer, 0, 0), **resident),
                pl.BlockSpec((1, F, D), lambda i: (layer, 0, 0), **resident)]
    args = [x, mods, norm_g, w_in, w_out]
    if mixer is not None:
        in_specs += [pl.BlockSpec((tm, y.shape[1]), lambda i: (i, 0)) for y in mixer[:3]]
        in_specs.append(pl.BlockSpec((1, D, D), lambda i: (layer, 0, 0), **resident))
        args += list(mixer)
    return pl.pallas_call(
        functools.partial(_ffn_kernel, mod_base=mod_base, chunk=chunk, with_mixer=mixer is not None),
        grid=(T // tm,),
        in_specs=in_specs,
        out_specs=pl.BlockSpec((tm, D), lambda i: (i, 0)),
        out_shape=jax.ShapeDtypeStruct((T, D), F32),
        scratch_shapes=[pltpu.VMEM((tm, F), BF16)],
        compiler_params=_cparams(("parallel",)),
        name="ffn",
    )(*args)


def _rope_tm(x, cos, sin_up, sin_dn, quarter):
    outs = []
    for c0 in range(0, x.shape[1], 128):
        xb = x[:, c0:c0 + 128]
        up = pltpu.roll(xb, 128 - quarter, 1)
        dn = pltpu.roll(xb, quarter, 1)
        outs.append(xb * cos + up * sin_up + dn * sin_dn)
    return outs[0] if len(outs) == 1 else jnp.concatenate(outs, axis=1)


def _head_rms_scale(x):
    return lax.rsqrt(jnp.mean(x * x, axis=-1, keepdims=True) + NORM_EPS)


def _per_head(scales, width):
    lane = lax.broadcasted_iota(jnp.int32, (1, width), 1)
    out = scales[-1]
    for h in range(len(scales) - 2, -1, -1):
        out = jnp.where(lane < (h + 1) * HEAD_W, scales[h], out)
    return out


def _seqmix_kernel(*refs, ts, seq_len, rope, kv_f32):
    it = iter(refs)
    xp_ref, xc_ref, xn_ref, mod_ref, g_ref, w_ref = (next(it) for _ in range(6))
    pw_ref, ps_ref, cw_ref, qn_ref, kn_ref = next(it), next(it), next(it), next(it), next(it)
    if rope:
        cg_ref, sgu_ref, sgd_ref, cd_ref, sdu_ref, sdd_ref = (next(it) for _ in range(6))
    ypc_ref, qg_ref, qd_ref, kg_ref, vg_ref, kd_ref, vd_ref = (next(it) for _ in range(7))

    j = pl.program_id(1)
    nt = pl.num_programs(1)
    n = ts + 2 * HALO
    mid = slice(HALO, HALO + ts)

    m = mod_ref[0]
    xe = jnp.concatenate([xp_ref[...], xc_ref[...], xn_ref[...]], axis=0)
    he = _norm_mod(xe, g_ref[0], m[3:4], m[4:5])
    w = w_ref[0]
    i = lax.broadcasted_iota(jnp.int32, (n, 1), 0)
    has_prev = (j > 0).astype(F32)
    has_next = (j < nt - 1).astype(F32)
    in_seq = jnp.where(i < HALO, has_prev, jnp.where(i >= HALO + ts, has_next, 1.0))
    proj_ext = _dot(he.astype(BF16), w[:, 0:C_QG]) * in_seq
    proj_att = _dot(he[mid].astype(BF16), w[:, C_QG:MIX_IN])

    def ext(c0):
        return proj_ext[:, c0:c0 + 256]

    def cur(c0, width):
        if c0 < C_QG:
            return proj_ext[mid, c0:c0 + width]
        return proj_att[:, c0 - C_QG:c0 - C_QG + width]

    ue = ext(C_POOL)
    up = lambda x, s: pltpu.roll(x, n - s, 0)
    dn = lambda x, s: pltpu.roll(x, s, 0)
    c2 = ue + dn(ue, 1)
    c4 = up(c2, 1) + dn(c2, 1)
    c8 = up(c4, 2) + dn(c4, 2)
    c16 = up(c8, 4) + dn(c8, 4)
    lane = lax.broadcasted_iota(jnp.int32, (1, POOL_WIDTH), 1)
    total = jnp.where(lane < 64, c2[mid],
                      jnp.where(lane < 128, c4[mid], jnp.where(lane < 192, c8[mid], c16[mid])))
    half = jnp.where(lane < 64, 1, jnp.where(lane < 128, 2, jnp.where(lane < 192, 4, 8)))
    pos = j * ts + lax.broadcasted_iota(jnp.int32, (ts, 1), 0)
    cnt = jnp.minimum(pos + half, seq_len) - jnp.maximum(pos - half, 0)
    u = cur(C_POOL, 256)
    p = total / cnt.astype(F32) - u
    y_pool = _dot(p.astype(BF16), pw_ref[0]) * ps_ref[0]

    ce = ext(C_CONVC) * ext(C_CONVH)
    cw = cw_ref[0]
    conv = (dn(ce, 1)[mid] * cw[0:1] + ce[mid] * cw[1:2] + up(ce, 1)[mid] * cw[2:3])
    y_conv = cur(C_CONVB, 256) * conv
    ypc_ref[...] = jnp.concatenate([y_pool, y_conv], axis=1).astype(BF16)

    q = cur(C_QG, 256)
    k = cur(C_KG, 128)
    qw = q * qn_ref[0]
    kw = k * kn_ref[0]
    if rope:
        qw = _rope_tm(qw, cg_ref[...], sgu_ref[...], sgd_ref[...], HEAD_W // 4)
        kw = _rope_tm(kw, cg_ref[...], sgu_ref[...], sgd_ref[...], HEAD_W // 4)
    heads = lambda x, n: [x[:, h * HEAD_W:(h + 1) * HEAD_W] for h in range(n)]
    q_scale = _per_head([_head_rms_scale(x) for x in heads(q, GQA_HEADS)], 256)
    k_scale = _per_head([_head_rms_scale(x) for x in heads(k, GQA_KV_HEADS)], 128)
    qg_ref[...] = (qw * (q_scale * (HEAD_W ** -0.5 * LOG2E))).T.astype(BF16)
    kn = kw * k_scale
    v = cur(C_VG, 128)

    qd = cur(C_QD, 256)
    kd = cur(C_KD, 256)
    vd = cur(C_VD, 256)
    if rope:
        qd = _rope_tm(qd, cd_ref[...], sdu_ref[...], sdd_ref[...], DIFF_HEAD_DIM // 4)
        kd = _rope_tm(kd, cd_ref[...], sdu_ref[...], sdd_ref[...], DIFF_HEAD_DIM // 4)
    qd_ref[...] = (qd * (DIFF_HEAD_DIM ** -0.5 * LOG2E)).T.astype(BF16)

    if kv_f32:
        for h, (kh, vh) in enumerate(zip(heads(kn, GQA_KV_HEADS), heads(v, GQA_KV_HEADS))):
            kg_ref[0, h] = kh
            vg_ref[0, h] = vh
        for h, (kh, vh) in enumerate(zip(heads(kd, DIFF_HEADS), heads(vd, DIFF_HEADS))):
            kd_ref[0, h] = kh
            vd_ref[0, h] = vh
    else:
        kg_ref[...] = kn.astype(BF16)
        kd_ref[...] = kd.astype(BF16)
        ones = jnp.ones((ts, HEAD_W), F32)
        for h, vh in enumerate(heads(v, GQA_KV_HEADS)):
            vg_ref[2 * HEAD_W * h:2 * HEAD_W * (h + 1), :] = jnp.concatenate([vh, ones], axis=1).T.astype(BF16)
        for h, vh in enumerate(heads(vd, DIFF_HEADS)):
            vd_ref[2 * HEAD_W * h:2 * HEAD_W * (h + 1), :] = jnp.concatenate([vh, ones], axis=1).T.astype(BF16)


def _seqmix(x, mods, norm_g, w_mix_in, pool_w_bd, pool_scale, conv_w, qn, kn, ropes, *,
            layer, batch, seq_len, kv_f32, row0, rows_per_cond):
    T, D = x.shape
    ts = _pick(seq_len, (512, 256, 128))
    nt = seq_len // ts
    hb = ts // HALO
    rope = ropes is not None
    tile = lambda b, j: b * nt + j
    in_specs = [
        pl.BlockSpec((HALO, D), lambda b, j: (jnp.maximum(tile(b, j) * hb - 1, 0), 0)),
        pl.BlockSpec((ts, D), lambda b, j: (tile(b, j), 0)),
        pl.BlockSpec((HALO, D), lambda b, j: (jnp.minimum((tile(b, j) + 1) * hb, T // HALO - 1), 0)),
        pl.BlockSpec((1, N_MOD, D), lambda b, j: (layer * COND_ROWS + row0 + (b * seq_len) // rows_per_cond, 0, 0)),
        pl.BlockSpec((1, 1, D), lambda b, j: (layer, 0, 0)),
        pl.BlockSpec((1, D, MIX_IN), lambda b, j: (layer, 0, 0), pipeline_mode=pl.Buffered(1)),
        pl.BlockSpec((1, 256, 256), lambda b, j: (layer, 0, 0)),
        pl.BlockSpec((1, 1, 256), lambda b, j: (layer, 0, 0)),
        pl.BlockSpec((1, 3, 256), lambda b, j: (layer, 0, 0)),
        pl.BlockSpec((1, 1, 256), lambda b, j: (layer, 0, 0)),
        pl.BlockSpec((1, 1, 128), lambda b, j: (layer, 0, 0)),
    ]
    args = [x, x, x, mods, norm_g, w_mix_in, pool_w_bd, pool_scale, conv_w, qn, kn]
    if rope:
        in_specs += [pl.BlockSpec((ts, 128), lambda b, j: (j, 0))] * 6
        args += list(ropes)
    rows = lambda w: pl.BlockSpec((ts, w), lambda b, j: (tile(b, j), 0))
    cols = lambda h: pl.BlockSpec((h, ts), lambda b, j: (0, tile(b, j)))
    out_specs = [rows(512), cols(256), cols(256)]
    out_shape = [jax.ShapeDtypeStruct((T, 512), BF16),
                 jax.ShapeDtypeStruct((256, T), BF16),
                 jax.ShapeDtypeStruct((256, T), BF16)]
    for h, is_key in ((GQA_KV_HEADS, True), (GQA_KV_HEADS, False), (DIFF_HEADS, True), (DIFF_HEADS, False)):
        if kv_f32:
            out_specs.append(pl.BlockSpec((1, h, ts, HEAD_W), lambda b, j: (b, 0, j, 0)))
            out_shape.append(jax.ShapeDtypeStruct((batch, h, seq_len, HEAD_W), F32))
        elif is_key:
            out_specs.append(rows(h * HEAD_W))
            out_shape.append(jax.ShapeDtypeStruct((T, h * HEAD_W), BF16))
        else:
            out_specs.append(cols(h * 2 * HEAD_W))
            out_shape.append(jax.ShapeDtypeStruct((h * 2 * HEAD_W, T), BF16))
    return pl.pallas_call(
        functools.partial(_seqmix_kernel, ts=ts, seq_len=seq_len, rope=rope, kv_f32=kv_f32),
        grid=(batch, nt),
        in_specs=in_specs,
        out_specs=out_specs,
        out_shape=out_shape,
        compiler_params=_cparams(("parallel", "parallel")),
        name="seqmix",
    )(*args)


def _attend_kernel(*refs, has_ctx, lam_init, tq, n_qb, n_keys):
    it = iter(refs)
    qg_ref, qd_ref, kg_ref, vg_ref, kd_ref, vd_ref = (next(it) for _ in range(6))
    if has_ctx:
        ckg_ref, cvg_ref, ckd_ref, cvd_ref = (next(it) for _ in range(4))
    lam_ref, sub_ref = next(it), next(it)
    yg_ref, yd_ref = next(it), next(it)
    s_ref, p_ref = next(it), next(it)

    def values_t(v):
        return jnp.concatenate([v, jnp.ones_like(v)], axis=1).T[0:VT_ROWS].astype(BF16)

    n_tot = n_keys + (ckg_ref.shape[3] if has_ctx else 0)
    row = lax.broadcasted_iota(jnp.int32, (HEAD_W, 1), 0)

    def gqa_qt(qb, kh):
        r, cols = 2 * HEAD_W * kh, slice(qb * tq, (qb + 1) * tq)
        return jnp.concatenate([qg_ref[r:r + HEAD_W, cols], qg_ref[r + HEAD_W:r + 2 * HEAD_W, cols]], axis=1)

    def diff_qt(qb, h):
        qh = qd_ref[HEAD_W * h:HEAD_W * (h + 1), qb * tq:(qb + 1) * tq]
        return jnp.concatenate([jnp.where(row < DIFF_HEAD_DIM, qh, jnp.zeros_like(qh)),
                                jnp.where(row >= DIFF_HEAD_DIM, qh, jnp.zeros_like(qh))], axis=1)

    ctx_g = (ckg_ref, cvg_ref) if has_ctx else (None, None)
    ctx_d = (ckd_ref, cvd_ref) if has_ctx else (None, None)
    per_qb = GQA_KV_HEADS + DIFF_HEADS
    units = []
    for qb in range(n_qb):
        units += [(functools.partial(gqa_qt, qb, kh), kg_ref, vg_ref, *ctx_g, kh) for kh in range(GQA_KV_HEADS)]
        units += [(functools.partial(diff_qt, qb, h), kd_ref, vd_ref, *ctx_d, h) for h in range(DIFF_HEADS)]
    m_acc = [None] * len(units)

    def score_steps(u):
        qt_fn, k_ref, _, ck_ref, _, h = units[u]
        qt = qt_fn()

        def put(r0, k, q):
            s = _dot(k, q)
            rows = s.shape[0]
            s_ref[u % 2, r0:r0 + rows, :] = s
            blk = jnp.max(s.reshape(rows // 8, 8, s.shape[1]), axis=0)
            m_acc[u] = blk if m_acc[u] is None else jnp.maximum(m_acc[u], blk)

        if not has_ctx:
            return [lambda: put(0, k_ref[0, h].astype(BF16), qt)]
        zeros = jnp.zeros_like(qt)
        qt128 = jnp.concatenate([qt, zeros] if h % 2 == 0 else [zeros, qt], axis=0)
        lanes = slice(128 * (h // 2), 128 * (h // 2) + 128)
        steps = [functools.partial(lambda r0: put(r0, k_ref[r0:r0 + SCORE_CHUNK, lanes], qt128), r0)
                 for r0 in range(0, n_keys, SCORE_CHUNK)]
        return steps + [lambda: put(n_keys, ck_ref[0, 0, h].astype(BF16), qt)]

    def exp_steps(u):
        m = jnp.max(m_acc[u], axis=0, keepdims=True)

        def block(r0):
            p_ref[u % 2, r0:r0 + EXP_CHUNK, :] = jnp.exp2(s_ref[u % 2, r0:r0 + EXP_CHUNK, :] - m).astype(BF16)

        return [functools.partial(block, r0) for r0 in range(0, n_tot, EXP_CHUNK)]

    def weighted_values(u):
        _, _, v_ref, _, cv_ref, h = units[u]
        if has_ctx:
            o = _dot(v_ref[2 * HEAD_W * h:2 * HEAD_W * h + VT_ROWS, :], p_ref[u % 2, 0:n_keys, :])
            o = o + _dot(values_t(cv_ref[0, 0, h]), p_ref[u % 2, n_keys:n_tot, :])
        else:
            o = _dot(values_t(v_ref[0, h]), p_ref[u % 2])
        return o[0:HEAD_W] / o[HEAD_W:HEAD_W + 1]

    for step in score_steps(0):
        step()
    outs = []
    for u in range(len(units)):
        exps = exp_steps(u)
        nxt = score_steps(u + 1) if u + 1 < len(units) else []
        per = -(-len(exps) // max(len(nxt), 1))
        for i in range(max(len(nxt), 1)):
            if i < len(nxt):
                nxt[i]()
            for e in exps[i * per:(i + 1) * per]:
                e()
        outs.append(weighted_values(u))

    lv = lam_ref[0]
    lam = (jnp.exp(jnp.sum(lv[0:1] * lv[1:2], axis=-1, keepdims=True))
           - jnp.exp(jnp.sum(lv[2:3] * lv[3:4], axis=-1, keepdims=True)) + lam_init)
    for qb in range(n_qb):
        o_qb = outs[qb * per_qb:(qb + 1) * per_qb]
        out_rows = slice(qb * tq, (qb + 1) * tq)
        yg = []
        for o in o_qb[:GQA_KV_HEADS]:
            yg += [o[:, 0:tq], o[:, tq:2 * tq]]
        yg_ref[out_rows, :] = jnp.concatenate(yg, axis=0).T.astype(BF16)
        yd = []
        for o in o_qb[GQA_KV_HEADS:]:
            o = o[:, 0:tq] - lam * o[:, tq:2 * tq]
            o = o * lax.rsqrt(jnp.mean(o * o, axis=0, keepdims=True) + NORM_EPS) * sub_ref[0]
            yd.append(o * (1.0 - lam_init))
        yd_ref[out_rows, :] = jnp.concatenate(yd, axis=0).T.astype(BF16)


def _attend(qg, qd, kg, vg, kd, vd, ctx, diff_lambda, diff_subln, *, layer, batch, seq_len, lam_init):
    T = qg.shape[1]
    has_ctx = ctx is not None
    tq = _pick(seq_len, (256, 128))
    n_qb = _pick(seq_len // tq, (ATTEND_QB, 1))
    nq = seq_len // (tq * n_qb)
    qspec = pl.BlockSpec((256, n_qb * tq), lambda b, i: (0, b * nq + i))
    if has_ctx:
        once = dict(pipeline_mode=pl.Buffered(1))
        kspec = lambda h: pl.BlockSpec((seq_len, h * HEAD_W), lambda b, i: (b, 0), **once)
        vspec = lambda h: pl.BlockSpec((h * 2 * HEAD_W, seq_len), lambda b, i: (0, b), **once)
    else:
        kspec = vspec = lambda h: pl.BlockSpec((1, h, seq_len, HEAD_W), lambda b, i: (b, 0, 0, 0))
    in_specs = [qspec, qspec,
                kspec(GQA_KV_HEADS), vspec(GQA_KV_HEADS), kspec(DIFF_HEADS), vspec(DIFF_HEADS)]
    args = [qg, qd, kg, vg, kd, vd]
    n_tot = seq_len
    if has_ctx:
        n_tot += ctx[0].shape[3]
        for c in ctx:
            _, _, h, p, w = c.shape
            in_specs.append(pl.BlockSpec((1, 1, h, p, w), lambda b, i: (b, layer, 0, 0, 0)))
            args.append(c)
    in_specs += [pl.BlockSpec((1, 4, DIFF_HEAD_DIM), lambda b, i: (layer, 0, 0)),
                 pl.BlockSpec((1, HEAD_W, 1), lambda b, i: (layer, 0, 0))]
    args += [diff_lambda, diff_subln]
    ospec = pl.BlockSpec((n_qb * tq, 256), lambda b, i: (b * nq + i, 0))
    return pl.pallas_call(
        functools.partial(_attend_kernel, has_ctx=has_ctx, lam_init=lam_init, tq=tq, n_qb=n_qb, n_keys=seq_len),
        grid=(batch, nq),
        in_specs=in_specs,
        out_specs=[ospec, ospec],
        out_shape=[jax.ShapeDtypeStruct((T, 256), BF16)] * 2,
        scratch_shapes=[pltpu.VMEM((2, n_tot, 2 * tq), F32), pltpu.VMEM((2, n_tot, 2 * tq), BF16)],
        compiler_params=_cparams(("parallel", "parallel")),
        name="attend",
    )(*args)


def _final_kernel(x_ref, g_ref, o_ref):
    x = x_ref[...]
    ms = jnp.mean(x * x, axis=-1, keepdims=True)
    o_ref[...] = x * lax.rsqrt(ms + NORM_EPS) * g_ref[...]


def _final_norm(x, g):
    T, D = x.shape
    tm = _pick(T, (1024, 512, 256, 128))
    return pl.pallas_call(
        _final_kernel,
        grid=(T // tm,),
        in_specs=[pl.BlockSpec((tm, D), lambda i: (i, 0)), pl.BlockSpec((1, D), lambda i: (0, 0))],
        out_specs=pl.BlockSpec((tm, D), lambda i: (i, 0)),
        out_shape=jax.ShapeDtypeStruct((T, D), F32),
        compiler_params=_cparams(("parallel",)),
        name="final_norm",
    )(x, g)


def _rope_tables(seq, dim):
    rows = seq // GRID_W
    row = jnp.broadcast_to(jnp.arange(rows)[:, None], (rows, GRID_W)).reshape(seq).astype(F32)
    col = jnp.broadcast_to(jnp.arange(GRID_W)[None, :], (rows, GRID_W)).reshape(seq).astype(F32)
    quarter = dim // 4
    inv = 1.0 / (ROPE_THETA ** (jnp.arange(quarter, dtype=F32) / quarter))
    ar = row[:, None] * inv
    ac = col[:, None] * inv
    ang = jnp.tile(jnp.concatenate([ar, ar, ac, ac], axis=-1), (1, 128 // dim))
    chunk = (jnp.arange(128) // quarter) % 2
    cos, sin = jnp.cos(ang), jnp.sin(ang)
    return cos, jnp.where(chunk == 0, -sin, 0.0), jnp.where(chunk == 1, sin, 0.0)


def kernel(x_prompt, x_sample, cache_gqa_k, cache_gqa_v, cache_diff_k, cache_diff_v, c, c_ctx, w_ada, b_ada, norm_ffn1, norm_mix, norm_ffn2, w_ffn1_in, w_ffn1_out, w_ffn2_in, w_ffn2_out, w_mix_in, pool_w, pool_scale, conv_w, gqa_q_norm, gqa_k_norm, diff_lambda, diff_subln, w_mix_out, final_norm):
    bp, sp, D = x_prompt.shape
    bs, ss, _ = x_sample.shape
    L = w_ada.shape[0]
    assert 1 + bs <= COND_ROWS and ss % GRID_W == 0

    cond = jnp.zeros((COND_ROWS, D), F32).at[0].set(c_ctx).at[1:1 + bs].set(c)
    mods = _ada(cond, w_ada, b_ada[:, None, :]).reshape(L * COND_ROWS, N_MOD, D)

    w1i, w1o = w_ffn1_in.astype(BF16), w_ffn1_out.astype(BF16)
    w2i, w2o = w_ffn2_in.astype(BF16), w_ffn2_out.astype(BF16)
    wmi, wmo = w_mix_in.astype(BF16), w_mix_out.astype(BF16)
    g = pool_w.shape[1]
    pw_bd = jnp.einsum('lgcd,gh->lgchd', pool_w, jnp.eye(g, dtype=F32)).reshape(L, POOL_WIDTH, POOL_WIDTH).astype(BF16)
    n1, nm, n2 = norm_ffn1[:, None, :], norm_mix[:, None, :], norm_ffn2[:, None, :]
    ps = pool_scale[:, None, :]
    qn = jnp.tile(gqa_q_norm, (1, GQA_HEADS))[:, None, :]
    kn = jnp.tile(gqa_k_norm, (1, GQA_KV_HEADS))[:, None, :]
    sub = diff_subln[:, :, None]
    ropes = _rope_tables(ss, HEAD_W) + _rope_tables(ss, DIFF_HEAD_DIM)
    ctx = (cache_gqa_k, cache_gqa_v, cache_diff_k, cache_diff_v)
    lam_inits = [0.8 - 0.6 * math.exp(-0.3 * l) for l in range(L)]

    def run_stream(x, batch, seq_len, row0, rows_per_cond, is_ctx_stream):
        kvs = []
        for l in range(L):
            cond_kw = dict(layer=l, row0=row0, rows_per_cond=rows_per_cond)
            x = _ffn(x, mods, n1, w1i, w1o, mod_base=0, **cond_kw)
            ypc, qg, qd, kg, vg, kd, vd = _seqmix(
                x, mods, nm, wmi, pw_bd, ps, conv_w, qn, kn, None if is_ctx_stream else ropes,
                batch=batch, seq_len=seq_len, kv_f32=is_ctx_stream, **cond_kw)
            yg, yd = _attend(qg, qd, kg, vg, kd, vd, None if is_ctx_stream else ctx, diff_lambda, sub,
                             layer=l, batch=batch, seq_len=seq_len, lam_init=lam_inits[l])
            x = _ffn(x, mods, n2, w2i, w2o, (ypc, yg, yd, wmo), mod_base=6, **cond_kw)
            kvs.append((kg, vg, kd, vd))
        return _final_norm(x, final_norm[None, :]), kvs

    yp, kvs = run_stream(x_prompt.reshape(bp * sp, D), bp, sp, 0, bp * sp, True)
    ys, _ = run_stream(x_sample.reshape(bs * ss, D), bs, ss, 1, ss, False)
    new_kv = [jnp.stack([kv[i] for kv in kvs], axis=1) for i in range(4)]
    return (yp.reshape(bp, sp, D), ys.reshape(bs, ss, D), *new_kv)
```

```python
import functools
import math

import jax
import jax.numpy as jnp
from jax import lax
from jax.experimental import pallas as pl
from jax.experimental.pallas import tpu as pltpu

F32 = jnp.float32
BF16 = jnp.bfloat16

N_MOD = 9
NORM_EPS = 1e-6
ROPE_THETA = 10000.0
GRID_W = 64
POOL_WINDOWS = (2, 4, 8, 16)
POOL_WIDTH = 256
CONV_WIDTH = 256
GQA_HEADS = 4
GQA_KV_HEADS = 2
GQA_GROUP = GQA_HEADS // GQA_KV_HEADS
HEAD_W = 64
DIFF_HEADS = 4
DIFF_HEAD_DIM = 32
HALO = 8
COND_ROWS = 16
LOG2E = math.log2(math.e)
SCORE_CHUNK = 512
EXP_CHUNK = 128
VT_ROWS = 80
ATTEND_QB = 2
FFN_TM = 512
FFN_CHUNK = 256

C_POOL, C_CONVH, C_CONVB, C_CONVC = 0, 256, 512, 768
C_QG, C_KG, C_VG, C_QD, C_KD, C_VD = 1024, 1280, 1408, 1536, 1792, 2048
MIX_IN = 2304

VMEM_LIMIT_BYTES = 56 * 1024 * 1024


def _cparams(sem):
    return pltpu.CompilerParams(dimension_semantics=sem, vmem_limit_bytes=VMEM_LIMIT_BYTES)


def _pick(n, cands):
    for c in cands:
        if n % c == 0:
            return c
    raise ValueError(f"no tile in {cands} divides {n}")


def _dot(a, b):
    return jnp.dot(a, b, preferred_element_type=F32)


def _dot_nt(a, b):
    return lax.dot_general(a, b, (((1,), (1,)), ((), ())), preferred_element_type=F32)


def _norm_mod(x, g, shift, scale):
    ms = jnp.mean(x * x, axis=-1, keepdims=True)
    return (x * lax.rsqrt(ms + NORM_EPS) * g) * (1.0 + scale) + shift


def _ada_kernel(c_ref, w_ref, b_ref, o_ref):
    c = c_ref[...]
    a = (c * jax.nn.sigmoid(c)).astype(BF16)
    o_ref[0] = _dot(a, w_ref[0].astype(BF16)) + b_ref[0]


def _ada(cond, w_ada, b_ada):
    L, D, N = w_ada.shape
    R = cond.shape[0]
    tn = _pick(N, (1024, 512, 256, 128))
    return pl.pallas_call(
        _ada_kernel,
        grid=(L, N // tn),
        in_specs=[pl.BlockSpec((R, D), lambda l, j: (0, 0)),
                  pl.BlockSpec((1, D, tn), lambda l, j: (l, 0, j)),
                  pl.BlockSpec((1, 1, tn), lambda l, j: (l, 0, j))],
        out_specs=pl.BlockSpec((1, R, tn), lambda l, j: (l, 0, j)),
        out_shape=jax.ShapeDtypeStruct((L, R, N), F32),
        compiler_params=_cparams(("parallel", "parallel")),
        name="ada",
    )(cond, w_ada, b_ada)


def _ffn_kernel(*refs, mod_base, chunk, with_mixer):
    it = iter(refs)
    x_ref, mod_ref, g_ref, wi_ref, wo_ref = (next(it) for _ in range(5))
    if with_mixer:
        ypc_ref, yg_ref, yd_ref, wm_ref = (next(it) for _ in range(4))
    o_ref, a_ref = next(it), next(it)
    m = mod_ref[0]
    x = x_ref[...]
    if with_mixer:
        wm = wm_ref[0]
        n_pc, n_g = ypc_ref.shape[1], yg_ref.shape[1]
        y = (_dot(ypc_ref[...], wm[0:n_pc]) + _dot(yg_ref[...], wm[n_pc:n_pc + n_g])
             + _dot(yd_ref[...], wm[n_pc + n_g:]))
        x = x + m[mod_base - 1:mod_base] * y
    h = _norm_mod(x, g_ref[0], m[mod_base:mod_base + 1], m[mod_base + 1:mod_base + 2]).astype(BF16)
    ffn_dim = wo_ref.shape[1]
    for c0 in range(0, ffn_dim, chunk):
        g = _dot(h, wi_ref[0, :, c0:c0 + chunk])
        u = _dot(h, wi_ref[0, :, ffn_dim + c0:ffn_dim + c0 + chunk])
        a_ref[:, c0:c0 + chunk] = (g * jax.nn.sigmoid(g) * u).astype(BF16)
    y = _dot(a_ref[...], wo_ref[0])
    o_ref[...] = x + (0.5 * m[mod_base + 2:mod_base + 3]) * y


def _ffn(x, mods, norm_g, w_in, w_out, mixer=None, *, layer, mod_base, row0, rows_per_cond):
    T, D = x.shape
    F = w_out.shape[1]
    tm = _pick(T, (FFN_TM, 256, 128))
    chunk = _pick(F, (FFN_CHUNK, 128))
    mrow = lambda i: (layer * COND_ROWS + row0 + (i * tm) // rows_per_cond, 0, 0)
    resident = dict(pipeline_mode=pl.Buffered(1))
    in_specs = [pl.BlockSpec((tm, D), lambda i: (i, 0)),
                pl.BlockSpec((1, N_MOD, D), mrow),
                pl.BlockSpec((1, 1, D), lambda i: (layer, 0, 0)),
                pl.BlockSpec((1, D, 2 * F), lambda i: (layer, 0, 0), **resident),
                pl.BlockSpec((1, F, D), lambda i: (layer, 0, 0), **resident)]
    args = [x, mods, norm_g, w_in, w_out]
    if mixer is not None:
        in_specs += [pl.BlockSpec((tm, y.shape[1]), lambda i: (i, 0)) for y in mixer[:3]]
        in_specs.append(pl.BlockSpec((1, D, D), lambda i: (layer, 0, 0), **resident))
        args += list(mixer)
    return pl.pallas_call(
        functools.partial(_ffn_kernel, mod_base=mod_base, chunk=chunk, with_mixer=mixer is not None),
        grid=(T // tm,),
        in_specs=in_specs,
        out_specs=pl.BlockSpec((tm, D), lambda i: (i, 0)),
        out_shape=jax.ShapeDtypeStruct((T, D), F32),
        scratch_shapes=[pltpu.VMEM((tm, F), BF16)],
        compiler_params=_cparams(("parallel",)),
        name="ffn",
    )(*args)


def _rope_tm(x, cos, sin_up, sin_dn, quarter):
    outs = []
    for c0 in range(0, x.shape[1], 128):
        xb = x[:, c0:c0 + 128]
        up = pltpu.roll(xb, 128 - quarter, 1)
        dn = pltpu.roll(xb, quarter, 1)
        outs.append(xb * cos + up * sin_up + dn * sin_dn)
    return outs[0] if len(outs) == 1 else jnp.concatenate(outs, axis=1)


def _head_rms_scale(x):
    return lax.rsqrt(jnp.mean(x * x, axis=-1, keepdims=True) + NORM_EPS)


def _per_head(scales, width):
    lane = lax.broadcasted_iota(jnp.int32, (1, width), 1)
    out = scales[-1]
    for h in range(len(scales) - 2, -1, -1):
        out = jnp.where(lane < (h + 1) * HEAD_W, scales[h], out)
    return out


def _seqmix_kernel(*refs, ts, seq_len, rope, kv_f32):
    it = iter(refs)
    xp_ref, xc_ref, xn_ref, mod_ref, g_ref, w_ref = (next(it) for _ in range(6))
    pw_ref, ps_ref, cw_ref, qn_ref, kn_ref = next(it), next(it), next(it), next(it), next(it)
    if rope:
        cg_ref, sgu_ref, sgd_ref, cd_ref, sdu_ref, sdd_ref = (next(it) for _ in range(6))
    ypc_ref, qg_ref, qd_ref, kg_ref, vg_ref, kd_ref, vd_ref = (next(it) for _ in range(7))

    j = pl.program_id(1)
    nt = pl.num_programs(1)
    n = ts + 2 * HALO
    mid = slice(HALO, HALO + ts)

    m = mod_ref[0]
    xe = jnp.concatenate([xp_ref[...], xc_ref[...], xn_ref[...]], axis=0)
    he = _norm_mod(xe, g_ref[0], m[3:4], m[4:5])
    w = w_ref[0]
    i = lax.broadcasted_iota(jnp.int32, (n, 1), 0)
    has_prev = (j > 0).astype(F32)
    has_next = (j < nt - 1).astype(F32)
    in_seq = jnp.where(i < HALO, has_prev, jnp.where(i >= HALO + ts, has_next, 1.0))
    h_att = he[mid].astype(BF16)
    heads = lambda x, n: [x[:, h * HEAD_W:(h + 1) * HEAD_W] for h in range(n)]

    proj_d = _dot(h_att, w[:, C_QD:MIX_IN])
    qd, kd, vd = proj_d[:, 0:256], proj_d[:, 256:512], proj_d[:, 512:768]
    if rope:
        qd = _rope_tm(qd, cd_ref[...], sdu_ref[...], sdd_ref[...], DIFF_HEAD_DIM // 4)
        kd = _rope_tm(kd, cd_ref[...], sdu_ref[...], sdd_ref[...], DIFF_HEAD_DIM // 4)
    qd_ref[...] = (qd * (DIFF_HEAD_DIM ** -0.5 * LOG2E)).T.astype(BF16)

    proj_g = _dot(h_att, w[:, C_QG:C_QD])
    q, k, v = proj_g[:, 0:256], proj_g[:, 256:384], proj_g[:, 384:512]
    qw = q * qn_ref[0]
    kw = k * kn_ref[0]
    if rope:
        qw = _rope_tm(qw, cg_ref[...], sgu_ref[...], sgd_ref[...], HEAD_W // 4)
        kw = _rope_tm(kw, cg_ref[...], sgu_ref[...], sgd_ref[...], HEAD_W // 4)
    q_scale = _per_head([_head_rms_scale(x) for x in heads(q, GQA_HEADS)], 256)
    k_scale = _per_head([_head_rms_scale(x) for x in heads(k, GQA_KV_HEADS)], 128)
    qg_ref[...] = (qw * (q_scale * (HEAD_W ** -0.5 * LOG2E))).T.astype(BF16)
    kn = kw * k_scale

    if kv_f32:
        for h, (kh, vh) in enumerate(zip(heads(kn, GQA_KV_HEADS), heads(v, GQA_KV_HEADS))):
            kg_ref[0, h] = kh
            vg_ref[0, h] = vh
        for h, (kh, vh) in enumerate(zip(heads(kd, DIFF_HEADS), heads(vd, DIFF_HEADS))):
            kd_ref[0, h] = kh
            vd_ref[0, h] = vh
    else:
        kg_ref[...] = kn.astype(BF16)
        kd_ref[...] = kd.astype(BF16)
        ones = jnp.ones((ts, HEAD_W), F32)
        for h, vh in enumerate(heads(v, GQA_KV_HEADS)):
            vg_ref[2 * HEAD_W * h:2 * HEAD_W * (h + 1), :] = jnp.concatenate([vh, ones], axis=1).T.astype(BF16)
        for h, vh in enumerate(heads(vd, DIFF_HEADS)):
            vd_ref[2 * HEAD_W * h:2 * HEAD_W * (h + 1), :] = jnp.concatenate([vh, ones], axis=1).T.astype(BF16)

    proj_ext = _dot(he.astype(BF16), w[:, 0:C_QG]) * in_seq
    ext = lambda c0: proj_ext[:, c0:c0 + 256]
    cur = lambda c0: proj_ext[mid, c0:c0 + 256]

    ue = ext(C_POOL)
    up = lambda x, s: pltpu.roll(x, n - s, 0)
    dn = lambda x, s: pltpu.roll(x, s, 0)
    c2 = ue + dn(ue, 1)
    c4 = up(c2, 1) + dn(c2, 1)
    c8 = up(c4, 2) + dn(c4, 2)
    c16 = up(c8, 4) + dn(c8, 4)
    lane = lax.broadcasted_iota(jnp.int32, (1, POOL_WIDTH), 1)
    total = jnp.where(lane < 64, c2[mid],
                      jnp.where(lane < 128, c4[mid], jnp.where(lane < 192, c8[mid], c16[mid])))
    half = jnp.where(lane < 64, 1, jnp.where(lane < 128, 2, jnp.where(lane < 192, 4, 8)))
    pos = j * ts + lax.broadcasted_iota(jnp.int32, (ts, 1), 0)
    cnt = jnp.minimum(pos + half, seq_len) - jnp.maximum(pos - half, 0)
    p = total / cnt.astype(F32) - cur(C_POOL)
    y_pool = _dot(p.astype(BF16), pw_ref[0]) * ps_ref[0]

    ce = ext(C_CONVC) * ext(C_CONVH)
    cw = cw_ref[0]
    conv = (dn(ce, 1)[mid] * cw[0:1] + ce[mid] * cw[1:2] + up(ce, 1)[mid] * cw[2:3])
    y_conv = cur(C_CONVB) * conv
    ypc_ref[...] = jnp.concatenate([y_pool, y_conv], axis=1).astype(BF16)


def _seqmix(x, mods, norm_g, w_mix_in, pool_w_bd, pool_scale, conv_w, qn, kn, ropes, *,
            layer, batch, seq_len, kv_f32, row0, rows_per_cond):
    T, D = x.shape
    ts = _pick(seq_len, (512, 256, 128))
    nt = seq_len // ts
    hb = ts // HALO
    rope = ropes is not None
    tile = lambda b, j: b * nt + j
    in_specs = [
        pl.BlockSpec((HALO, D), lambda b, j: (jnp.maximum(tile(b, j) * hb - 1, 0), 0)),
        pl.BlockSpec((ts, D), lambda b, j: (tile(b, j), 0)),
        pl.BlockSpec((HALO, D), lambda b, j: (jnp.minimum((tile(b, j) + 1) * hb, T // HALO - 1), 0)),
        pl.BlockSpec((1, N_MOD, D), lambda b, j: (layer * COND_ROWS + row0 + (b * seq_len) // rows_per_cond, 0, 0)),
        pl.BlockSpec((1, 1, D), lambda b, j: (layer, 0, 0)),
        pl.BlockSpec((1, D, MIX_IN), lambda b, j: (layer, 0, 0), pipeline_mode=pl.Buffered(1)),
        pl.BlockSpec((1, 256, 256), lambda b, j: (layer, 0, 0)),
        pl.BlockSpec((1, 1, 256), lambda b, j: (layer, 0, 0)),
        pl.BlockSpec((1, 3, 256), lambda b, j: (layer, 0, 0)),
        pl.BlockSpec((1, 1, 256), lambda b, j: (layer, 0, 0)),
        pl.BlockSpec((1, 1, 128), lambda b, j: (layer, 0, 0)),
    ]
    args = [x, x, x, mods, norm_g, w_mix_in, pool_w_bd, pool_scale, conv_w, qn, kn]
    if rope:
        in_specs += [pl.BlockSpec((ts, 128), lambda b, j: (j, 0))] * 6
        args += list(ropes)
    rows = lambda w: pl.BlockSpec((ts, w), lambda b, j: (tile(b, j), 0))
    cols = lambda h: pl.BlockSpec((h, ts), lambda b, j: (0, tile(b, j)))
    out_specs = [rows(512), cols(256), cols(256)]
    out_shape = [jax.ShapeDtypeStruct((T, 512), BF16),
                 jax.ShapeDtypeStruct((256, T), BF16),
                 jax.ShapeDtypeStruct((256, T), BF16)]
    for h, is_key in ((GQA_KV_HEADS, True), (GQA_KV_HEADS, False), (DIFF_HEADS, True), (DIFF_HEADS, False)):
        if kv_f32:
            out_specs.append(pl.BlockSpec((1, h, ts, HEAD_W), lambda b, j: (b, 0, j, 0)))
            out_shape.append(jax.ShapeDtypeStruct((batch, h, seq_len, HEAD_W), F32))
        elif is_key:
            out_specs.append(rows(h * HEAD_W))
            out_shape.append(jax.ShapeDtypeStruct((T, h * HEAD_W), BF16))
        else:
            out_specs.append(cols(h * 2 * HEAD_W))
            out_shape.append(jax.ShapeDtypeStruct((h * 2 * HEAD_W, T), BF16))
    return pl.pallas_call(
        functools.partial(_seqmix_kernel, ts=ts, seq_len=seq_len, rope=rope, kv_f32=kv_f32),
        grid=(batch, nt),
        in_specs=in_specs,
        out_specs=out_specs,
        out_shape=out_shape,
        compiler_params=_cparams(("parallel", "parallel")),
        name="seqmix",
    )(*args)


def _attend_kernel(*refs, has_ctx, lam_init, tq, n_qb, n_keys):
    it = iter(refs)
    qg_ref, qd_ref, kg_ref, vg_ref, kd_ref, vd_ref = (next(it) for _ in range(6))
    if has_ctx:
        ckg_ref, cvg_ref, ckd_ref, cvd_ref = (next(it) for _ in range(4))
    lam_ref, sub_ref = next(it), next(it)
    yg_ref, yd_ref = next(it), next(it)
    s_ref, p_ref = next(it), next(it)

    def values_t(v):
        return jnp.concatenate([v, jnp.ones_like(v)], axis=1).T[0:VT_ROWS].astype(BF16)

    n_tot = n_keys + (ckg_ref.shape[3] if has_ctx else 0)
    row = lax.broadcasted_iota(jnp.int32, (HEAD_W, 1), 0)

    def gqa_qt(qb, kh):
        r, cols = 2 * HEAD_W * kh, slice(qb * tq, (qb + 1) * tq)
        return jnp.concatenate([qg_ref[r:r + HEAD_W, cols], qg_ref[r + HEAD_W:r + 2 * HEAD_W, cols]], axis=1)

    def diff_qt(qb, h):
        qh = qd_ref[HEAD_W * h:HEAD_W * (h + 1), qb * tq:(qb + 1) * tq]
        return jnp.concatenate([jnp.where(row < DIFF_HEAD_DIM, qh, jnp.zeros_like(qh)),
                                jnp.where(row >= DIFF_HEAD_DIM, qh, jnp.zeros_like(qh))], axis=1)

    ctx_g = (ckg_ref, cvg_ref) if has_ctx else (None, None)
    ctx_d = (ckd_ref, cvd_ref) if has_ctx else (None, None)
    per_qb = GQA_KV_HEADS + DIFF_HEADS
    units = []
    for qb in range(n_qb):
        units += [(functools.partial(gqa_qt, qb, kh), kg_ref, vg_ref, *ctx_g, kh) for kh in range(GQA_KV_HEADS)]
        units += [(functools.partial(diff_qt, qb, h), kd_ref, vd_ref, *ctx_d, h) for h in range(DIFF_HEADS)]
    m_acc = [None] * len(units)

    def score_steps(u):
        qt_fn, k_ref, _, ck_ref, _, h = units[u]
        qt = qt_fn()

        def put(r0, k, q):
            s = _dot(k, q)
            rows = s.shape[0]
            s_ref[u % 2, r0:r0 + rows, :] = s
            blk = jnp.max(s.reshape(rows // 8, 8, s.shape[1]), axis=0)
            m_acc[u] = blk if m_acc[u] is None else jnp.maximum(m_acc[u], blk)

        if not has_ctx:
            return [lambda: put(0, k_ref[0, h].astype(BF16), qt)]
        zeros = jnp.zeros_like(qt)
        qt128 = jnp.concatenate([qt, zeros] if h % 2 == 0 else [zeros, qt], axis=0)
        lanes = slice(128 * (h // 2), 128 * (h // 2) + 128)
        steps = [functools.partial(lambda r0: put(r0, k_ref[r0:r0 + SCORE_CHUNK, lanes], qt128), r0)
                 for r0 in range(0, n_keys, SCORE_CHUNK)]
        return steps + [lambda: put(n_keys, ck_ref[0, 0, h].astype(BF16), qt)]

    def exp_steps(u):
        m = jnp.max(m_acc[u], axis=0, keepdims=True)

        def block(r0):
            p_ref[u % 2, r0:r0 + EXP_CHUNK, :] = jnp.exp2(s_ref[u % 2, r0:r0 + EXP_CHUNK, :] - m).astype(BF16)

        return [functools.partial(block, r0) for r0 in range(0, n_tot, EXP_CHUNK)]

    def weighted_values(u):
        _, _, v_ref, _, cv_ref, h = units[u]
        if has_ctx:
            o = _dot(v_ref[2 * HEAD_W * h:2 * HEAD_W * h + VT_ROWS, :], p_ref[u % 2, 0:n_keys, :])
            o = o + _dot(values_t(cv_ref[0, 0, h]), p_ref[u % 2, n_keys:n_tot, :])
        else:
            o = _dot(values_t(v_ref[0, h]), p_ref[u % 2])
        return o[0:HEAD_W] / o[HEAD_W:HEAD_W + 1]

    for step in score_steps(0):
        step()
    outs = []
    for u in range(len(units)):
        exps = exp_steps(u)
        nxt = score_steps(u + 1) if u + 1 < len(units) else []
        per = -(-len(exps) // max(len(nxt), 1))
        for i in range(max(len(nxt), 1)):
            if i < len(nxt):
                nxt[i]()
            for e in exps[i * per:(i + 1) * per]:
                e()
        outs.append(weighted_values(u))

    lv = lam_ref[0]
    lam = (jnp.exp(jnp.sum(lv[0:1] * lv[1:2], axis=-1, keepdims=True))
           - jnp.exp(jnp.sum(lv[2:3] * lv[3:4], axis=-1, keepdims=True)) + lam_init)
    for qb in range(n_qb):
        o_qb = outs[qb * per_qb:(qb + 1) * per_qb]
        out_rows = slice(qb * tq, (qb + 1) * tq)
        yg = []
        for o in o_qb[:GQA_KV_HEADS]:
            yg += [o[:, 0:tq], o[:, tq:2 * tq]]
        yg_ref[out_rows, :] = jnp.concatenate(yg, axis=0).T.astype(BF16)
        yd = []
        for o in o_qb[GQA_KV_HEADS:]:
            o = o[:, 0:tq] - lam * o[:, tq:2 * tq]
            o = o * lax.rsqrt(jnp.mean(o * o, axis=0, keepdims=True) + NORM_EPS) * sub_ref[0]
            yd.append(o * (1.0 - lam_init))
        yd_ref[out_rows, :] = jnp.concatenate(yd, axis=0).T.astype(BF16)


def _attend(qg, qd, kg, vg, kd, vd, ctx, diff_lambda, diff_subln, *, layer, batch, seq_len, lam_init):
    T = qg.shape[1]
    has_ctx = ctx is not None
    tq = _pick(seq_len, (256, 128))
    n_qb = _pick(seq_len // tq, (ATTEND_QB, 1))
    nq = seq_len // (tq * n_qb)
    qspec = pl.BlockSpec((256, n_qb * tq), lambda b, i: (0, b * nq + i))
    if has_ctx:
        once = dict(pipeline_mode=pl.Buffered(1))
        kspec = lambda h: pl.BlockSpec((seq_len, h * HEAD_W), lambda b, i: (b, 0), **once)
        vspec = lambda h: pl.BlockSpec((h * 2 * HEAD_W, seq_len), lambda b, i: (0, b), **once)
    else:
        kspec = vspec = lambda h: pl.BlockSpec((1, h, seq_len, HEAD_W), lambda b, i: (b, 0, 0, 0))
    in_specs = [qspec, qspec,
                kspec(GQA_KV_HEADS), vspec(GQA_KV_HEADS), kspec(DIFF_HEADS), vspec(DIFF_HEADS)]
    args = [qg, qd, kg, vg, kd, vd]
    n_tot = seq_len
    if has_ctx:
        n_tot += ctx[0].shape[3]
        for c in ctx:
            _, _, h, p, w = c.shape
            in_specs.append(pl.BlockSpec((1, 1, h, p, w), lambda b, i: (b, layer, 0, 0, 0)))
            args.append(c)
    in_specs += [pl.BlockSpec((1, 4, DIFF_HEAD_DIM), lambda b, i: (layer, 0, 0)),
                 pl.BlockSpec((1, HEAD_W, 1), lambda b, i: (layer, 0, 0))]
    args += [diff_lambda, diff_subln]
    ospec = pl.BlockSpec((n_qb * tq, 256), lambda b, i: (b * nq + i, 0))
    return pl.pallas_call(
        functools.partial(_attend_kernel, has_ctx=has_ctx, lam_init=lam_init, tq=tq, n_qb=n_qb, n_keys=seq_len),
        grid=(batch, nq),
        in_specs=in_specs,
        out_specs=[ospec, ospec],
        out_shape=[jax.ShapeDtypeStruct((T, 256), BF16)] * 2,
        scratch_shapes=[pltpu.VMEM((2, n_tot, 2 * tq), F32), pltpu.VMEM((2, n_tot, 2 * tq), BF16)],
        compiler_params=_cparams(("parallel", "parallel")),
        name="attend",
    )(*args)


def _final_kernel(x_ref, g_ref, o_ref):
    x = x_ref[...]
    ms = jnp.mean(x * x, axis=-1, keepdims=True)
    o_ref[...] = x * lax.rsqrt(ms + NORM_EPS) * g_ref[...]


def _final_norm(x, g):
    T, D = x.shape
    tm = _pick(T, (1024, 512, 256, 128))
    return pl.pallas_call(
        _final_kernel,
        grid=(T // tm,),
        in_specs=[pl.BlockSpec((tm, D), lambda i: (i, 0)), pl.BlockSpec((1, D), lambda i: (0, 0))],
        out_specs=pl.BlockSpec((tm, D), lambda i: (i, 0)),
        out_shape=jax.ShapeDtypeStruct((T, D), F32),
        compiler_params=_cparams(("parallel",)),
        name="final_norm",
    )(x, g)


def _rope_tables(seq, dim):
    rows = seq // GRID_W
    row = jnp.broadcast_to(jnp.arange(rows)[:, None], (rows, GRID_W)).reshape(seq).astype(F32)
    col = jnp.broadcast_to(jnp.arange(GRID_W)[None, :], (rows, GRID_W)).reshape(seq).astype(F32)
    quarter = dim // 4
    inv = 1.0 / (ROPE_THETA ** (jnp.arange(quarter, dtype=F32) / quarter))
    ar = row[:, None] * inv
    ac = col[:, None] * inv
    ang = jnp.tile(jnp.concatenate([ar, ar, ac, ac], axis=-1), (1, 128 // dim))
    chunk = (jnp.arange(128) // quarter) % 2
    cos, sin = jnp.cos(ang), jnp.sin(ang)
    return cos, jnp.where(chunk == 0, -sin, 0.0), jnp.where(chunk == 1, sin, 0.0)


def kernel(x_prompt, x_sample, cache_gqa_k, cache_gqa_v, cache_diff_k, cache_diff_v, c, c_ctx, w_ada, b_ada, norm_ffn1, norm_mix, norm_ffn2, w_ffn1_in, w_ffn1_out, w_ffn2_in, w_ffn2_out, w_mix_in, pool_w, pool_scale, conv_w, gqa_q_norm, gqa_k_norm, diff_lambda, diff_subln, w_mix_out, final_norm):
    bp, sp, D = x_prompt.shape
    bs, ss, _ = x_sample.shape
    L = w_ada.shape[0]
    assert 1 + bs <= COND_ROWS and ss % GRID_W == 0

    cond = jnp.zeros((COND_ROWS, D), F32).at[0].set(c_ctx).at[1:1 + bs].set(c)
    mods = _ada(cond, w_ada, b_ada[:, None, :]).reshape(L * COND_ROWS, N_MOD, D)

    w1i, w1o = w_ffn1_in.astype(BF16), w_ffn1_out.astype(BF16)
    w2i, w2o = w_ffn2_in.astype(BF16), w_ffn2_out.astype(BF16)
    wmi, wmo = w_mix_in.astype(BF16), w_mix_out.astype(BF16)
    g = pool_w.shape[1]
    pw_bd = jnp.einsum('lgcd,gh->lgchd', pool_w, jnp.eye(g, dtype=F32)).reshape(L, POOL_WIDTH, POOL_WIDTH).astype(BF16)
    n1, nm, n2 = norm_ffn1[:, None, :], norm_mix[:, None, :], norm_ffn2[:, None, :]
    ps = pool_scale[:, None, :]
    qn = jnp.tile(gqa_q_norm, (1, GQA_HEADS))[:, None, :]
    kn = jnp.tile(gqa_k_norm, (1, GQA_KV_HEADS))[:, None, :]
    sub = diff_subln[:, :, None]
    ropes = _rope_tables(ss, HEAD_W) + _rope_tables(ss, DIFF_HEAD_DIM)
    ctx = (cache_gqa_k, cache_gqa_v, cache_diff_k, cache_diff_v)
    lam_inits = [0.8 - 0.6 * math.exp(-0.3 * l) for l in range(L)]

    def run_stream(x, batch, seq_len, row0, rows_per_cond, is_ctx_stream):
        kvs = []
        for l in range(L):
            cond_kw = dict(layer=l, row0=row0, rows_per_cond=rows_per_cond)
            x = _ffn(x, mods, n1, w1i, w1o, mod_base=0, **cond_kw)
            ypc, qg, qd, kg, vg, kd, vd = _seqmix(
                x, mods, nm, wmi, pw_bd, ps, conv_w, qn, kn, None if is_ctx_stream else ropes,
                batch=batch, seq_len=seq_len, kv_f32=is_ctx_stream, **cond_kw)
            yg, yd = _attend(qg, qd, kg, vg, kd, vd, None if is_ctx_stream else ctx, diff_lambda, sub,
                             layer=l, batch=batch, seq_len=seq_len, lam_init=lam_inits[l])
            x = _ffn(x, mods, n2, w2i, w2o, (ypc, yg, yd, wmo), mod_base=6, **cond_kw)
            kvs.append((kg, vg, kd, vd))
        return _final_norm(x, final_norm[None, :]), kvs

    yp, kvs = run_stream(x_prompt.reshape(bp * sp, D), bp, sp, 0, bp * sp, True)
    ys, _ = run_stream(x_sample.reshape(bs * ss, D), bs, ss, 1, ss, False)
    new_kv = [jnp.stack([kv[i] for kv in kvs], axis=1) for i in range(4)]
    return (yp.reshape(bp, sp, D), ys.reshape(bs, ss, D), *new_kv)
```

```python
import functools
import math

import jax
import jax.numpy as jnp
from jax import lax
from jax.experimental import pallas as pl
from jax.experimental.pallas import tpu as pltpu

F32 = jnp.float32
BF16 = jnp.bfloat16

N_MOD = 9
NORM_EPS = 1e-6
ROPE_THETA = 10000.0
GRID_W = 64
POOL_WINDOWS = (2, 4, 8, 16)
POOL_WIDTH = 256
CONV_WIDTH = 256
GQA_HEADS = 4
GQA_KV_HEADS = 2
GQA_GROUP = GQA_HEADS // GQA_KV_HEADS
HEAD_W = 64
DIFF_HEADS = 4
DIFF_HEAD_DIM = 32
HALO = 8
COND_ROWS = 16
LOG2E = math.log2(math.e)
SCORE_CHUNK = 512
EXP_CHUNK = 128
VT_ROWS = 80
ATTEND_QB = 2
FFN_TM = 512
FFN_CHUNK = 256

C_POOL, C_CONVH, C_CONVB, C_CONVC = 0, 256, 512, 768
C_QG, C_KG, C_VG, C_QD, C_KD, C_VD = 1024, 1280, 1408, 1536, 1792, 2048
MIX_IN = 2304

VMEM_LIMIT_BYTES = 56 * 1024 * 1024


def _cparams(sem):
    return pltpu.CompilerParams(dimension_semantics=sem, vmem_limit_bytes=VMEM_LIMIT_BYTES)


def _pick(n, cands):
    for c in cands:
        if n % c == 0:
            return c
    raise ValueError(f"no tile in {cands} divides {n}")


def _dot(a, b):
    return jnp.dot(a, b, preferred_element_type=F32)


def _dot_nt(a, b):
    return lax.dot_general(a, b, (((1,), (1,)), ((), ())), preferred_element_type=F32)


def _norm_mod(x, g, shift, scale):
    ms = jnp.mean(x * x, axis=-1, keepdims=True)
    return (x * lax.rsqrt(ms + NORM_EPS) * g) * (1.0 + scale) + shift


def _ada_kernel(c_ref, w_ref, b_ref, o_ref):
    c = c_ref[...]
    a = (c * jax.nn.sigmoid(c)).astype(BF16)
    o_ref[0] = _dot(a, w_ref[0].astype(BF16)) + b_ref[0]


def _ada(cond, w_ada, b_ada):
    L, D, N = w_ada.shape
    R = cond.shape[0]
    tn = _pick(N, (1024, 512, 256, 128))
    return pl.pallas_call(
        _ada_kernel,
        grid=(L, N // tn),
        in_specs=[pl.BlockSpec((R, D), lambda l, j: (0, 0)),
                  pl.BlockSpec((1, D, tn), lambda l, j: (l, 0, j)),
                  pl.BlockSpec((1, 1, tn), lambda l, j: (l, 0, j))],
        out_specs=pl.BlockSpec((1, R, tn), lambda l, j: (l, 0, j)),
        out_shape=jax.ShapeDtypeStruct((L, R, N), F32),
        compiler_params=_cparams(("parallel", "parallel")),
        name="ada",
    )(cond, w_ada, b_ada)


def _ffn_kernel(*refs, mod_base, chunk, with_mixer):
    it = iter(refs)
    x_ref, mod_ref, g_ref, wi_ref, wo_ref = (next(it) for _ in range(5))
    if with_mixer:
        ypc_ref, yg_ref, yd_ref, wm_ref = (next(it) for _ in range(4))
    o_ref, a_ref = next(it), next(it)
    m = mod_ref[0]
    x = x_ref[...]
    if with_mixer:
        wm = wm_ref[0]
        n_pc, n_g = ypc_ref.shape[1], yg_ref.shape[1]
        y = (_dot(ypc_ref[...], wm[0:n_pc]) + _dot(yg_ref[...], wm[n_pc:n_pc + n_g])
             + _dot(yd_ref[...], wm[n_pc + n_g:]))
        x = x + m[mod_base - 1:mod_base] * y
    h = _norm_mod(x, g_ref[0], m[mod_base:mod_base + 1], m[mod_base + 1:mod_base + 2]).astype(BF16)
    ffn_dim = wo_ref.shape[1]
    for c0 in range(0, ffn_dim, chunk):
        g = _dot(h, wi_ref[0, :, c0:c0 + chunk])
        u = _dot(h, wi_ref[0, :, ffn_dim + c0:ffn_dim + c0 + chunk])
        a_ref[:, c0:c0 + chunk] = (g * jax.nn.sigmoid(g) * u).astype(BF16)
    y = _dot(a_ref[...], wo_ref[0])
    o_ref[...] = x + (0.5 * m[mod_base + 2:mod_base + 3]) * y


def _ffn(x, mods, norm_g, w_in, w_out, mixer=None, *, layer, mod_base, row0, rows_per_cond):
    T, D = x.shape
    F = w_out.shape[1]
    tm = _pick(T, (FFN_TM, 256, 128))
    chunk = _pick(F, (FFN_CHUNK, 128))
    mrow = lambda i: (layer * COND_ROWS + row0 + (i * tm) // rows_per_cond, 0, 0)
    resident = dict(pipeline_mode=pl.Buffered(1))
    in_specs = [pl.BlockSpec((tm, D), lambda i: (i, 0)),
                pl.BlockSpec((1, N_MOD, D), mrow),
                pl.BlockSpec((1, 1, D), lambda i: (layer, 0, 0)),
                pl.BlockSpec((1, D, 2 * F), lambda i: (layer, 0, 0), **resident),
                pl.BlockSpec((1, F, D), lambda i: (layer, 0, 0), **resident)]
    args = [x, mods, norm_g, w_in, w_out]
    if mixer is not None:
        in_specs += [pl.BlockSpec((tm, y.shape[1]), lambda i: (i, 0)) for y in mixer[:3]]
        in_specs.append(pl.BlockSpec((1, D, D), lambda i: (layer, 0, 0), **resident))
        args += list(mixer)
    return pl.pallas_call(
        functools.partial(_ffn_kernel, mod_base=mod_base, chunk=chunk, with_mixer=mixer is not None),
        grid=(T // tm,),
        in_specs=in_specs,
        out_specs=pl.BlockSpec((tm, D), lambda i: (i, 0)),
        out_shape=jax.ShapeDtypeStruct((T, D), F32),
        scratch_shapes=[pltpu.VMEM((tm, F), BF16)],
        compiler_params=_cparams(("parallel",)),
        name="ffn",
    )(*args)


def _rope_tm(x, cos, sin, rot):
    outs = []
    for c0 in range(0, x.shape[1], 128):
        xb = x[:, c0:c0 + 128]
        hi = xb.astype(BF16)
        lo = (xb - hi.astype(F32)).astype(BF16)
        outs.append(xb * cos + (_dot(hi, rot) + _dot(lo, rot)) * sin)
    return outs[0] if len(outs) == 1 else jnp.concatenate(outs, axis=1)


def _head_rms_scale(x):
    return lax.rsqrt(jnp.mean(x * x, axis=-1, keepdims=True) + NORM_EPS)


def _per_head(scales, width):
    lane = lax.broadcasted_iota(jnp.int32, (1, width), 1)
    out = scales[-1]
    for h in range(len(scales) - 2, -1, -1):
        out = jnp.where(lane < (h + 1) * HEAD_W, scales[h], out)
    return out


def _seqmix_kernel(*refs, ts, seq_len, rope, kv_f32):
    it = iter(refs)
    xp_ref, xc_ref, xn_ref, mod_ref, g_ref, w_ref = (next(it) for _ in range(6))
    pw_ref, ps_ref, cw_ref, qn_ref, kn_ref = next(it), next(it), next(it), next(it), next(it)
    if rope:
        cg_ref, sg_ref, rg_ref, cd_ref, sd_ref, rd_ref = (next(it) for _ in range(6))
    ypc_ref, qg_ref, qd_ref, kg_ref, vg_ref, kd_ref, vd_ref = (next(it) for _ in range(7))

    j = pl.program_id(1)
    nt = pl.num_programs(1)
    n = ts + 2 * HALO
    mid = slice(HALO, HALO + ts)

    m = mod_ref[0]
    xe = jnp.concatenate([xp_ref[...], xc_ref[...], xn_ref[...]], axis=0)
    he = _norm_mod(xe, g_ref[0], m[3:4], m[4:5])
    w = w_ref[0]
    i = lax.broadcasted_iota(jnp.int32, (n, 1), 0)
    has_prev = (j > 0).astype(F32)
    has_next = (j < nt - 1).astype(F32)
    in_seq = jnp.where(i < HALO, has_prev, jnp.where(i >= HALO + ts, has_next, 1.0))
    h_att = he[mid].astype(BF16)
    heads = lambda x, n: [x[:, h * HEAD_W:(h + 1) * HEAD_W] for h in range(n)]

    proj_d = _dot(h_att, w[:, C_QD:MIX_IN])
    qd, kd, vd = proj_d[:, 0:256], proj_d[:, 256:512], proj_d[:, 512:768]
    if rope:
        qd = _rope_tm(qd, cd_ref[...], sd_ref[...], rd_ref[...])
        kd = _rope_tm(kd, cd_ref[...], sd_ref[...], rd_ref[...])
    qd_ref[...] = (qd * (DIFF_HEAD_DIM ** -0.5 * LOG2E)).T.astype(BF16)

    proj_g = _dot(h_att, w[:, C_QG:C_QD])
    q, k, v = proj_g[:, 0:256], proj_g[:, 256:384], proj_g[:, 384:512]
    qw = q * qn_ref[0]
    kw = k * kn_ref[0]
    if rope:
        qw = _rope_tm(qw, cg_ref[...], sg_ref[...], rg_ref[...])
        kw = _rope_tm(kw, cg_ref[...], sg_ref[...], rg_ref[...])
    q_scale = _per_head([_head_rms_scale(x) for x in heads(q, GQA_HEADS)], 256)
    k_scale = _per_head([_head_rms_scale(x) for x in heads(k, GQA_KV_HEADS)], 128)
    qg_ref[...] = (qw * (q_scale * (HEAD_W ** -0.5 * LOG2E))).T.astype(BF16)
    kn = kw * k_scale

    if kv_f32:
        for h, (kh, vh) in enumerate(zip(heads(kn, GQA_KV_HEADS), heads(v, GQA_KV_HEADS))):
            kg_ref[0, h] = kh
            vg_ref[0, h] = vh
        for h, (kh, vh) in enumerate(zip(heads(kd, DIFF_HEADS), heads(vd, DIFF_HEADS))):
            kd_ref[0, h] = kh
            vd_ref[0, h] = vh
    else:
        kg_ref[...] = kn.astype(BF16)
        kd_ref[...] = kd.astype(BF16)
        ones = jnp.ones((ts, HEAD_W), F32)
        for h, vh in enumerate(heads(v, GQA_KV_HEADS)):
            vg_ref[2 * HEAD_W * h:2 * HEAD_W * (h + 1), :] = jnp.concatenate([vh, ones], axis=1).T.astype(BF16)
        for h, vh in enumerate(heads(vd, DIFF_HEADS)):
            vd_ref[2 * HEAD_W * h:2 * HEAD_W * (h + 1), :] = jnp.concatenate([vh, ones], axis=1).T.astype(BF16)

    proj_ext = _dot(he.astype(BF16), w[:, 0:C_QG]) * in_seq
    ext = lambda c0: proj_ext[:, c0:c0 + 256]
    cur = lambda c0: proj_ext[mid, c0:c0 + 256]

    ue = ext(C_POOL)
    up = lambda x, s: pltpu.roll(x, n - s, 0)
    dn = lambda x, s: pltpu.roll(x, s, 0)
    c2 = ue + dn(ue, 1)
    c4 = up(c2, 1) + dn(c2, 1)
    c8 = up(c4, 2) + dn(c4, 2)
    c16 = up(c8, 4) + dn(c8, 4)
    lane = lax.broadcasted_iota(jnp.int32, (1, POOL_WIDTH), 1)
    total = jnp.where(lane < 64, c2[mid],
                      jnp.where(lane < 128, c4[mid], jnp.where(lane < 192, c8[mid], c16[mid])))
    half = jnp.where(lane < 64, 1, jnp.where(lane < 128, 2, jnp.where(lane < 192, 4, 8)))
    pos = j * ts + lax.broadcasted_iota(jnp.int32, (ts, 1), 0)
    cnt = jnp.minimum(pos + half, seq_len) - jnp.maximum(pos - half, 0)
    p = total / cnt.astype(F32) - cur(C_POOL)
    y_pool = _dot(p.astype(BF16), pw_ref[0]) * ps_ref[0]

    ce = ext(C_CONVC) * ext(C_CONVH)
    cw = cw_ref[0]
    conv = (dn(ce, 1)[mid] * cw[0:1] + ce[mid] * cw[1:2] + up(ce, 1)[mid] * cw[2:3])
    y_conv = cur(C_CONVB) * conv
    ypc_ref[...] = jnp.concatenate([y_pool, y_conv], axis=1).astype(BF16)


def _seqmix(x, mods, norm_g, w_mix_in, pool_w_bd, pool_scale, conv_w, qn, kn, ropes, *,
            layer, batch, seq_len, kv_f32, row0, rows_per_cond):
    T, D = x.shape
    ts = _pick(seq_len, (512, 256, 128))
    nt = seq_len // ts
    hb = ts // HALO
    rope = ropes is not None
    tile = lambda b, j: b * nt + j
    in_specs = [
        pl.BlockSpec((HALO, D), lambda b, j: (jnp.maximum(tile(b, j) * hb - 1, 0), 0)),
        pl.BlockSpec((ts, D), lambda b, j: (tile(b, j), 0)),
        pl.BlockSpec((HALO, D), lambda b, j: (jnp.minimum((tile(b, j) + 1) * hb, T // HALO - 1), 0)),
        pl.BlockSpec((1, N_MOD, D), lambda b, j: (layer * COND_ROWS + row0 + (b * seq_len) // rows_per_cond, 0, 0)),
        pl.BlockSpec((1, 1, D), lambda b, j: (layer, 0, 0)),
        pl.BlockSpec((1, D, MIX_IN), lambda b, j: (layer, 0, 0), pipeline_mode=pl.Buffered(1)),
        pl.BlockSpec((1, 256, 256), lambda b, j: (layer, 0, 0)),
        pl.BlockSpec((1, 1, 256), lambda b, j: (layer, 0, 0)),
        pl.BlockSpec((1, 3, 256), lambda b, j: (layer, 0, 0)),
        pl.BlockSpec((1, 1, 256), lambda b, j: (layer, 0, 0)),
        pl.BlockSpec((1, 1, 128), lambda b, j: (layer, 0, 0)),
    ]
    args = [x, x, x, mods, norm_g, w_mix_in, pool_w_bd, pool_scale, conv_w, qn, kn]
    if rope:
        table = pl.BlockSpec((ts, 128), lambda b, j: (j, 0))
        perm = pl.BlockSpec((128, 128), lambda b, j: (0, 0))
        in_specs += [table, table, perm] * 2
        args += list(ropes)
    rows = lambda w: pl.BlockSpec((ts, w), lambda b, j: (tile(b, j), 0))
    cols = lambda h: pl.BlockSpec((h, ts), lambda b, j: (0, tile(b, j)))
    out_specs = [rows(512), cols(256), cols(256)]
    out_shape = [jax.ShapeDtypeStruct((T, 512), BF16),
                 jax.ShapeDtypeStruct((256, T), BF16),
                 jax.ShapeDtypeStruct((256, T), BF16)]
    for h, is_key in ((GQA_KV_HEADS, True), (GQA_KV_HEADS, False), (DIFF_HEADS, True), (DIFF_HEADS, False)):
        if kv_f32:
            out_specs.append(pl.BlockSpec((1, h, ts, HEAD_W), lambda b, j: (b, 0, j, 0)))
            out_shape.append(jax.ShapeDtypeStruct((batch, h, seq_len, HEAD_W), F32))
        elif is_key:
            out_specs.append(rows(h * HEAD_W))
            out_shape.append(jax.ShapeDtypeStruct((T, h * HEAD_W), BF16))
        else:
            out_specs.append(cols(h * 2 * HEAD_W))
            out_shape.append(jax.ShapeDtypeStruct((h * 2 * HEAD_W, T), BF16))
    return pl.pallas_call(
        functools.partial(_seqmix_kernel, ts=ts, seq_len=seq_len, rope=rope, kv_f32=kv_f32),
        grid=(batch, nt),
        in_specs=in_specs,
        out_specs=out_specs,
        out_shape=out_shape,
        compiler_params=_cparams(("parallel", "parallel")),
        name="seqmix",
    )(*args)


def _attend_kernel(*refs, has_ctx, lam_init, tq, n_qb, n_keys):
    it = iter(refs)
    qg_ref, qd_ref, kg_ref, vg_ref, kd_ref, vd_ref = (next(it) for _ in range(6))
    if has_ctx:
        ckg_ref, cvg_ref, ckd_ref, cvd_ref = (next(it) for _ in range(4))
    lam_ref, sub_ref = next(it), next(it)
    yg_ref, yd_ref = next(it), next(it)
    s_ref, p_ref = next(it), next(it)

    def values_t(v):
        return jnp.concatenate([v, jnp.ones_like(v)], axis=1).T[0:VT_ROWS].astype(BF16)

    n_tot = n_keys + (ckg_ref.shape[3] if has_ctx else 0)
    row = lax.broadcasted_iota(jnp.int32, (HEAD_W, 1), 0)

    def gqa_qt(qb, kh):
        r, cols = 2 * HEAD_W * kh, slice(qb * tq, (qb + 1) * tq)
        return jnp.concatenate([qg_ref[r:r + HEAD_W, cols], qg_ref[r + HEAD_W:r + 2 * HEAD_W, cols]], axis=1)

    def diff_qt(qb, h):
        qh = qd_ref[HEAD_W * h:HEAD_W * (h + 1), qb * tq:(qb + 1) * tq]
        return jnp.concatenate([jnp.where(row < DIFF_HEAD_DIM, qh, jnp.zeros_like(qh)),
                                jnp.where(row >= DIFF_HEAD_DIM, qh, jnp.zeros_like(qh))], axis=1)

    ctx_g = (ckg_ref, cvg_ref) if has_ctx else (None, None)
    ctx_d = (ckd_ref, cvd_ref) if has_ctx else (None, None)
    per_qb = GQA_KV_HEADS + DIFF_HEADS
    units = []
    for qb in range(n_qb):
        units += [(functools.partial(gqa_qt, qb, kh), kg_ref, vg_ref, *ctx_g, kh) for kh in range(GQA_KV_HEADS)]
        units += [(functools.partial(diff_qt, qb, h), kd_ref, vd_ref, *ctx_d, h) for h in range(DIFF_HEADS)]
    m_acc = [None] * len(units)

    def score_steps(u):
        qt_fn, k_ref, _, ck_ref, _, h = units[u]
        qt = qt_fn()

        def put(r0, k, q):
            s = _dot(k, q)
            rows = s.shape[0]
            s_ref[u % 2, r0:r0 + rows, :] = s
            blk = jnp.max(s.reshape(rows // 8, 8, s.shape[1]), axis=0)
            m_acc[u] = blk if m_acc[u] is None else jnp.maximum(m_acc[u], blk)

        if not has_ctx:
            return [lambda: put(0, k_ref[0, h].astype(BF16), qt)]
        zeros = jnp.zeros_like(qt)
        qt128 = jnp.concatenate([qt, zeros] if h % 2 == 0 else [zeros, qt], axis=0)
        lanes = slice(128 * (h // 2), 128 * (h // 2) + 128)
        steps = [functools.partial(lambda r0: put(r0, k_ref[r0:r0 + SCORE_CHUNK, lanes], qt128), r0)
                 for r0 in range(0, n_keys, SCORE_CHUNK)]
        return steps + [lambda: put(n_keys, ck_ref[0, 0, h].astype(BF16), qt)]

    def exp_steps(u):
        m = jnp.max(m_acc[u], axis=0, keepdims=True)

        def block(r0):
            p_ref[u % 2, r0:r0 + EXP_CHUNK, :] = jnp.exp2(s_ref[u % 2, r0:r0 + EXP_CHUNK, :] - m).astype(BF16)

        return [functools.partial(block, r0) for r0 in range(0, n_tot, EXP_CHUNK)]

    def weighted_values(u):
        _, _, v_ref, _, cv_ref, h = units[u]
        if has_ctx:
            o = _dot(v_ref[2 * HEAD_W * h:2 * HEAD_W * h + VT_ROWS, :], p_ref[u % 2, 0:n_keys, :])
            o = o + _dot(values_t(cv_ref[0, 0, h]), p_ref[u % 2, n_keys:n_tot, :])
        else:
            o = _dot(values_t(v_ref[0, h]), p_ref[u % 2])
        return o[0:HEAD_W] / o[HEAD_W:HEAD_W + 1]

    for step in score_steps(0):
        step()
    outs = []
    for u in range(len(units)):
        exps = exp_steps(u)
        nxt = score_steps(u + 1) if u + 1 < len(units) else []
        per = -(-len(exps) // max(len(nxt), 1))
        for i in range(max(len(nxt), 1)):
            if i < len(nxt):
                nxt[i]()
            for e in exps[i * per:(i + 1) * per]:
                e()
        outs.append(weighted_values(u))

    lv = lam_ref[0]
    lam = (jnp.exp(jnp.sum(lv[0:1] * lv[1:2], axis=-1, keepdims=True))
           - jnp.exp(jnp.sum(lv[2:3] * lv[3:4], axis=-1, keepdims=True)) + lam_init)
    for qb in range(n_qb):
        o_qb = outs[qb * per_qb:(qb + 1) * per_qb]
        out_rows = slice(qb * tq, (qb + 1) * tq)
        yg = []
        for o in o_qb[:GQA_KV_HEADS]:
            yg += [o[:, 0:tq], o[:, tq:2 * tq]]
        yg_ref[out_rows, :] = jnp.concatenate(yg, axis=0).T.astype(BF16)
        yd = []
        for o in o_qb[GQA_KV_HEADS:]:
            o = o[:, 0:tq] - lam * o[:, tq:2 * tq]
            o = o * lax.rsqrt(jnp.mean(o * o, axis=0, keepdims=True) + NORM_EPS) * sub_ref[0]
            yd.append(o * (1.0 - lam_init))
        yd_ref[out_rows, :] = jnp.concatenate(yd, axis=0).T.astype(BF16)


def _attend(qg, qd, kg, vg, kd, vd, ctx, diff_lambda, diff_subln, *, layer, batch, seq_len, lam_init):
    T = qg.shape[1]
    has_ctx = ctx is not None
    tq = _pick(seq_len, (256, 128))
    n_qb = _pick(seq_len // tq, (ATTEND_QB, 1))
    nq = seq_len // (tq * n_qb)
    qspec = pl.BlockSpec((256, n_qb * tq), lambda b, i: (0, b * nq + i))
    if has_ctx:
        once = dict(pipeline_mode=pl.Buffered(1))
        kspec = lambda h: pl.BlockSpec((seq_len, h * HEAD_W), lambda b, i: (b, 0), **once)
        vspec = lambda h: pl.BlockSpec((h * 2 * HEAD_W, seq_len), lambda b, i: (0, b), **once)
    else:
        kspec = vspec = lambda h: pl.BlockSpec((1, h, seq_len, HEAD_W), lambda b, i: (b, 0, 0, 0))
    in_specs = [qspec, qspec,
                kspec(GQA_KV_HEADS), vspec(GQA_KV_HEADS), kspec(DIFF_HEADS), vspec(DIFF_HEADS)]
    args = [qg, qd, kg, vg, kd, vd]
    n_tot = seq_len
    if has_ctx:
        n_tot += ctx[0].shape[3]
        for c in ctx:
            _, _, h, p, w = c.shape
            in_specs.append(pl.BlockSpec((1, 1, h, p, w), lambda b, i: (b, layer, 0, 0, 0)))
            args.append(c)
    in_specs += [pl.BlockSpec((1, 4, DIFF_HEAD_DIM), lambda b, i: (layer, 0, 0)),
                 pl.BlockSpec((1, HEAD_W, 1), lambda b, i: (layer, 0, 0))]
    args += [diff_lambda, diff_subln]
    ospec = pl.BlockSpec((n_qb * tq, 256), lambda b, i: (b * nq + i, 0))
    return pl.pallas_call(
        functools.partial(_attend_kernel, has_ctx=has_ctx, lam_init=lam_init, tq=tq, n_qb=n_qb, n_keys=seq_len),
        grid=(batch, nq),
        in_specs=in_specs,
        out_specs=[ospec, ospec],
        out_shape=[jax.ShapeDtypeStruct((T, 256), BF16)] * 2,
        scratch_shapes=[pltpu.VMEM((2, n_tot, 2 * tq), F32), pltpu.VMEM((2, n_tot, 2 * tq), BF16)],
        compiler_params=_cparams(("parallel", "parallel")),
        name="attend",
    )(*args)


def _final_kernel(x_ref, g_ref, o_ref):
    x = x_ref[...]
    ms = jnp.mean(x * x, axis=-1, keepdims=True)
    o_ref[...] = x * lax.rsqrt(ms + NORM_EPS) * g_ref[...]


def _final_norm(x, g):
    T, D = x.shape
    tm = _pick(T, (1024, 512, 256, 128))
    return pl.pallas_call(
        _final_kernel,
        grid=(T // tm,),
        in_specs=[pl.BlockSpec((tm, D), lambda i: (i, 0)), pl.BlockSpec((1, D), lambda i: (0, 0))],
        out_specs=pl.BlockSpec((tm, D), lambda i: (i, 0)),
        out_shape=jax.ShapeDtypeStruct((T, D), F32),
        compiler_params=_cparams(("parallel",)),
        name="final_norm",
    )(x, g)


def _rope_tables(seq, dim):
    rows = seq // GRID_W
    row = jnp.broadcast_to(jnp.arange(rows)[:, None], (rows, GRID_W)).reshape(seq).astype(F32)
    col = jnp.broadcast_to(jnp.arange(GRID_W)[None, :], (rows, GRID_W)).reshape(seq).astype(F32)
    quarter = dim // 4
    inv = 1.0 / (ROPE_THETA ** (jnp.arange(quarter, dtype=F32) / quarter))
    ar = row[:, None] * inv
    ac = col[:, None] * inv
    ang = jnp.tile(jnp.concatenate([ar, ar, ac, ac], axis=-1), (1, 128 // dim))
    lane = jnp.arange(128)
    even = (lane // quarter) % 2 == 0
    src = jnp.where(even, lane + quarter, lane - quarter)
    rot = jnp.zeros((128, 128), F32).at[src, lane].set(jnp.where(even, -1.0, 1.0)).astype(BF16)
    return jnp.cos(ang), jnp.sin(ang), rot


def kernel(x_prompt, x_sample, cache_gqa_k, cache_gqa_v, cache_diff_k, cache_diff_v, c, c_ctx, w_ada, b_ada, norm_ffn1, norm_mix, norm_ffn2, w_ffn1_in, w_ffn1_out, w_ffn2_in, w_ffn2_out, w_mix_in, pool_w, pool_scale, conv_w, gqa_q_norm, gqa_k_norm, diff_lambda, diff_subln, w_mix_out, final_norm):
    bp, sp, D = x_prompt.shape
    bs, ss, _ = x_sample.shape
    L = w_ada.shape[0]
    assert 1 + bs <= COND_ROWS and ss % GRID_W == 0

    cond = jnp.zeros((COND_ROWS, D), F32).at[0].set(c_ctx).at[1:1 + bs].set(c)
    mods = _ada(cond, w_ada, b_ada[:, None, :]).reshape(L * COND_ROWS, N_MOD, D)

    w1i, w1o = w_ffn1_in.astype(BF16), w_ffn1_out.astype(BF16)
    w2i, w2o = w_ffn2_in.astype(BF16), w_ffn2_out.astype(BF16)
    wmi, wmo = w_mix_in.astype(BF16), w_mix_out.astype(BF16)
    g = pool_w.shape[1]
    pw_bd = jnp.einsum('lgcd,gh->lgchd', pool_w, jnp.eye(g, dtype=F32)).reshape(L, POOL_WIDTH, POOL_WIDTH).astype(BF16)
    n1, nm, n2 = norm_ffn1[:, None, :], norm_mix[:, None, :], norm_ffn2[:, None, :]
    ps = pool_scale[:, None, :]
    qn = jnp.tile(gqa_q_norm, (1, GQA_HEADS))[:, None, :]
    kn = jnp.tile(gqa_k_norm, (1, GQA_KV_HEADS))[:, None, :]
    sub = diff_subln[:, :, None]
    ropes = _rope_tables(ss, HEAD_W) + _rope_tables(ss, DIFF_HEAD_DIM)
    ctx = (cache_gqa_k, cache_gqa_v, cache_diff_k, cache_diff_v)
    lam_inits = [0.8 - 0.6 * math.exp(-0.3 * l) for l in range(L)]

    def run_stream(x, batch, seq_len, row0, rows_per_cond, is_ctx_stream):
        kvs = []
        for l in range(L):
            cond_kw = dict(layer=l, row0=row0, rows_per_cond=rows_per_cond)
            x = _ffn(x, mods, n1, w1i, w1o, mod_base=0, **cond_kw)
            ypc, qg, qd, kg, vg, kd, vd = _seqmix(
                x, mods, nm, wmi, pw_bd, ps, conv_w, qn, kn, None if is_ctx_stream else ropes,
                batch=batch, seq_len=seq_len, kv_f32=is_ctx_stream, **cond_kw)
            yg, yd = _attend(qg, qd, kg, vg, kd, vd, None if is_ctx_stream else ctx, diff_lambda, sub,
                             layer=l, batch=batch, seq_len=seq_len, lam_init=lam_inits[l])
            x = _ffn(x, mods, n2, w2i, w2o, (ypc, yg, yd, wmo), mod_base=6, **cond_kw)
            kvs.append((kg, vg, kd, vd))
        return _final_norm(x, final_norm[None, :]), kvs

    yp, kvs = run_stream(x_prompt.reshape(bp * sp, D), bp, sp, 0, bp * sp, True)
    ys, _ = run_stream(x_sample.reshape(bs * ss, D), bs, ss, 1, ss, False)
    new_kv = [jnp.stack([kv[i] for kv in kvs], axis=1) for i in range(4)]
    return (yp.reshape(bp, sp, D), ys.reshape(bs, ss, D), *new_kv)
```
